```python
import math
import jax, jax.numpy as jnp
from jax import lax
import numpy as np

D_MODEL = 1024
BATCH = 8
SEQ = 8192
DEPTH = 1

CTX_LEN = 256
GRID_W = 64
ROPE_THETA = 10000.0
Q_BLOCK = 128
LN_EPS = 1e-5
RMS_EPS = 1e-6
SUBLN_EPS = 1e-5

MLA_HEADS = 8
MLA_NOPE = 64
MLA_ROPE = 32
MLA_V = 64
MLA_Q_LORA = 384
MLA_KV_LORA = 256
MLA_WIDTH = MLA_HEADS * MLA_V
MLA_SCALE = (MLA_NOPE + MLA_ROPE) ** -0.5

DIFF_HEADS = 4
DIFF_HD = 64
DIFF_WIDTH = DIFF_HEADS * 2 * DIFF_HD
DIFF_SCALE = DIFF_HD ** -0.5

IN_SPLITS = (
    MLA_Q_LORA,
    MLA_KV_LORA,
    MLA_ROPE,
    MLA_WIDTH,
    DIFF_WIDTH,
    DIFF_WIDTH,
    DIFF_WIDTH,
    DIFF_WIDTH,
    2 * D_MODEL,
)
N_IN = sum(IN_SPLITS)
IN_OFFSETS = tuple(int(o) for o in np.cumsum(IN_SPLITS)[:-1])

ALPHA = (2 * DEPTH) ** 0.25
BETA = (8 * DEPTH) ** -0.25

kernel_name = "hybrid_mla_diffattn_dit_layer"


def _rms_norm(x, w, eps):
    x32 = x.astype(jnp.float32)
    y = x32 * lax.rsqrt(jnp.mean(x32 * x32, axis=-1, keepdims=True) + eps)
    return y.astype(x.dtype) * w


def _layer_norm(x, g, b):
    x32 = x.astype(jnp.float32)
    mu = jnp.mean(x32, axis=-1, keepdims=True)
    var = jnp.mean(jnp.square(x32 - mu), axis=-1, keepdims=True)
    return ((x32 - mu) * lax.rsqrt(var + LN_EPS)).astype(x.dtype) * g + b


def _rope_1d(x, pos):
    half = x.shape[-1] // 2
    inv = ROPE_THETA ** (-jnp.arange(half, dtype=jnp.float32) / half)
    ang = pos[:, None] * inv[None, :]
    cos = jnp.cos(ang).astype(x.dtype)
    sin = jnp.sin(ang).astype(x.dtype)
    x1, x2 = x[..., :half], x[..., half:]
    return jnp.concatenate([x1 * cos - x2 * sin, x1 * sin + x2 * cos], axis=-1)


def _rope_2d(x, row, col):
    half = x.shape[-1] // 2
    return jnp.concatenate([_rope_1d(x[..., :half], row), _rope_1d(x[..., half:], col)], axis=-1)


def _blockwise(fn, qs):
    b, h, s, _ = qs[0].shape
    blocks = tuple(q.reshape(b, h, s // Q_BLOCK, Q_BLOCK, q.shape[-1]).transpose(2, 0, 1, 3, 4) for q in qs)
    out = lax.map(lambda qb: fn(*qb), blocks)
    n = out.shape[0]
    return out.transpose(1, 2, 0, 3, 4).reshape(b, h, n * Q_BLOCK, out.shape[-1])


def _softmax_attend(q, k, v, scale):
    s = jnp.einsum('bhqd,bhkd->bhqk', q, k).astype(jnp.float32) * scale
    p = jax.nn.softmax(s, axis=-1).astype(v.dtype)
    return jnp.einsum('bhqk,bhkd->bhqd', p, v)


def _diff_attend(q1, q2, k1, k2, v, lam, scale):
    s1 = jnp.einsum('bhqd,bhkd->bhqk', q1, k1).astype(jnp.float32) * scale
    s2 = jnp.einsum('bhqd,bhkd->bhqk', q2, k2).astype(jnp.float32) * scale
    p = jax.nn.softmax(s1, axis=-1) - lam * jax.nn.softmax(s2, axis=-1)
    return jnp.einsum('bhqk,bhkd->bhqd', p.astype(v.dtype), v)


def _branch_inputs(p, q_norm, kv_norm, w_uq, w_ukv, row, col):
    c_q, c_kv, k_r, gate_a, dq, dk, dv, gate_d, merge = jnp.split(p, IN_OFFSETS, axis=-1)
    b, s, _ = p.shape
    q = jnp.einsum('bsr,rhd->bhsd', _rms_norm(c_q, q_norm, RMS_EPS),
                   w_uq.reshape(MLA_Q_LORA, MLA_HEADS, MLA_NOPE + MLA_ROPE))
    kv = jnp.einsum('bsr,rhd->bhsd', _rms_norm(c_kv, kv_norm, RMS_EPS),
                    w_ukv.reshape(MLA_KV_LORA, MLA_HEADS, MLA_NOPE + MLA_V))
    q_nope, q_rope = q[..., :MLA_NOPE], q[..., MLA_NOPE:]
    k_nope, v_mla = kv[..., :MLA_NOPE], kv[..., MLA_NOPE:]
    k_r = k_r[:, None]
    dq = dq.reshape(b, s, DIFF_HEADS, 2, DIFF_HD).transpose(0, 2, 3, 1, 4)
    dk = dk.reshape(b, s, DIFF_HEADS, 2, DIFF_HD).transpose(0, 2, 3, 1, 4)
    dv = dv.reshape(b, s, DIFF_HEADS, 2 * DIFF_HD).transpose(0, 2, 1, 3)
    if row is not None:
        q_rope = _rope_2d(q_rope, row, col)
        k_r = _rope_2d(k_r, row, col)
        dq = _rope_2d(dq, row, col)
        dk = _rope_2d(dk, row, col)
    q_mla = jnp.concatenate([q_nope, q_rope], axis=-1)
    k_mla = jnp.concatenate([k_nope, jnp.broadcast_to(k_r, k_nope.shape[:-1] + (MLA_ROPE,))], axis=-1)
    return (q_mla, k_mla, v_mla, dq, dk, dv), (gate_a, gate_d, merge)


def _mix(q_mla, k_mla, v_mla, dq, dk, dv, gates, lam, lam_init, diff_subln, w_oa, w_ob, w_out):
    gate_a, gate_d, merge = gates
    b, s = gate_a.shape[0], gate_a.shape[1]
    o_a = _blockwise(lambda q: _softmax_attend(q, k_mla, v_mla, MLA_SCALE), (q_mla,))
    k1, k2 = dk[:, :, 0], dk[:, :, 1]
    o_d = _blockwise(lambda q1, q2: _diff_attend(q1, q2, k1, k2, dv, lam, DIFF_SCALE),
                     (dq[:, :, 0], dq[:, :, 1]))
    o_d = _rms_norm(o_d, diff_subln, SUBLN_EPS) * (1.0 - lam_init)
    o_a = o_a.transpose(0, 2, 1, 3).reshape(b, s, MLA_WIDTH)
    o_d = o_d.transpose(0, 2, 1, 3).reshape(b, s, DIFF_WIDTH)
    y_a = (o_a * jax.nn.silu(gate_a)) @ w_oa
    y_d = (o_d * jax.nn.silu(gate_d)) @ w_ob
    m_a, m_d = merge[..., :D_MODEL], merge[..., D_MODEL:]
    return (jax.nn.sigmoid(m_a) * y_a + jax.nn.sigmoid(m_d) * y_d) @ w_out


def setup_inputs(seed: int = 0) -> dict:
    key = jax.random.key(seed)
    ks = jax.random.split(key, 20)
    nrm = lambda k, shape, s: jax.random.normal(k, shape, jnp.float32) * s
    L, D = DEPTH, D_MODEL
    return {
        "x": nrm(ks[0], (BATCH, SEQ, D), 1.0),
        "c": nrm(ks[1], (BATCH, D), 1.0),
        "ctx": nrm(ks[2], (BATCH, CTX_LEN, D), 1.0),
        "c_ctx": nrm(ks[3], (D,), 1.0),
        "w_mod": nrm(ks[4], (L, D, 3 * D), 0.5 * D ** -0.5),
        "b_mod": nrm(ks[5], (L, 3 * D), 0.01),
        "w_in": nrm(ks[6], (L, D, N_IN), D ** -0.5),
        "mla_q_norm": 1.0 + nrm(ks[7], (L, MLA_Q_LORA), 0.01),
        "mla_kv_norm": 1.0 + nrm(ks[8], (L, MLA_KV_LORA), 0.01),
        "w_uq": nrm(ks[9], (L, MLA_Q_LORA, MLA_HEADS * (MLA_NOPE + MLA_ROPE)), MLA_Q_LORA ** -0.5),
        "w_ukv": nrm(ks[10], (L, MLA_KV_LORA, MLA_HEADS * (MLA_NOPE + MLA_V)), MLA_KV_LORA ** -0.5),
        "diff_lambda": nrm(ks[11], (L, 4, DIFF_HD), 0.1),
        "diff_subln": 1.0 + nrm(ks[12], (L, 2 * DIFF_HD), 0.01),
        "w_oa": nrm(ks[13], (L, MLA_WIDTH, D), BETA * MLA_WIDTH ** -0.5),
        "w_ob": nrm(ks[14], (L, DIFF_WIDTH, D), BETA * DIFF_WIDTH ** -0.5),
        "w_out": nrm(ks[15], (L, D, D), BETA * D ** -0.5),
        "ln_g": 1.0 + nrm(ks[16], (L, D), 0.01),
        "ln_b": nrm(ks[17], (L, D), 0.01),
    }


def reference(x, c, ctx, c_ctx, w_mod, b_mod, w_in, mla_q_norm, mla_kv_norm, w_uq, w_ukv,
              diff_lambda, diff_subln, w_oa, w_ob, w_out, ln_g, ln_b):
    n = x.shape[1]
    ROWS = n // GRID_W
    row = jnp.broadcast_to(jnp.arange(ROWS, dtype=jnp.float32)[:, None], (ROWS, GRID_W)).reshape(-1)
    col = jnp.broadcast_to(jnp.arange(GRID_W, dtype=jnp.float32)[None, :], (ROWS, GRID_W)).reshape(-1)
    for l in range(DEPTH):
        shift, scale, gate = jnp.split(jax.nn.silu(c) @ w_mod[l] + b_mod[l], 3, axis=-1)
        shift_c, scale_c, gate_c = jnp.split(jax.nn.silu(c_ctx) @ w_mod[l] + b_mod[l], 3, axis=-1)
        h = x * (1.0 + scale[:, None]) + shift[:, None]
        h_c = ctx * (1.0 + scale_c) + shift_c
        lat_t, lat_g = _branch_inputs(h @ w_in[l], mla_q_norm[l], mla_kv_norm[l], w_uq[l], w_ukv[l], row, col)
        ctx_t, ctx_g = _branch_inputs(h_c @ w_in[l], mla_q_norm[l], mla_kv_norm[l], w_uq[l], w_ukv[l], None, None)
        lam_init = 0.8 - 0.6 * math.exp(-0.3 * l)
        lq1, lk1, lq2, lk2 = diff_lambda[l, 0], diff_lambda[l, 1], diff_lambda[l, 2], diff_lambda[l, 3]
        lam = (jnp.exp(jnp.sum(lq1 * lk1).astype(jnp.float32))
               - jnp.exp(jnp.sum(lq2 * lk2).astype(jnp.float32)) + lam_init)
        q_mla, k_mla, v_mla, dq, dk, dv = lat_t
        q_mla_c, k_mla_c, v_mla_c, dq_c, dk_c, dv_c = ctx_t
        k_all = jnp.concatenate([k_mla_c, k_mla], axis=2)
        v_all = jnp.concatenate([v_mla_c, v_mla], axis=2)
        dk_all = jnp.concatenate([dk_c, dk], axis=3)
        dv_all = jnp.concatenate([dv_c, dv], axis=2)
        y = _mix(q_mla, k_all, v_all, dq, dk_all, dv_all, lat_g, lam, lam_init,
                 diff_subln[l], w_oa[l], w_ob[l], w_out[l])
        if l + 1 < DEPTH:
            y_c = _mix(q_mla_c, k_mla_c, v_mla_c, dq_c, dk_c, dv_c, ctx_g, lam, lam_init,
                       diff_subln[l], w_oa[l], w_ob[l], w_out[l])
            ctx = _layer_norm(ALPHA * ctx + gate_c * y_c, ln_g[l], ln_b[l])
        x = _layer_norm(ALPHA * x + gate[:, None] * y, ln_g[l], ln_b[l])
    return x
```

```python
import functools
import math

import numpy as np
import jax
import jax.numpy as jnp
from jax import lax
from jax.experimental import pallas as pl
from jax.experimental.pallas import tpu as pltpu

F32 = jnp.float32
BF16 = jnp.bfloat16

D_MODEL = 1024
GRID_W = 64
ROPE_THETA = 10000.0
LN_EPS = 1e-5
RMS_EPS = 1e-6
SUBLN_EPS = 1e-5
MLA_HEADS = 8
MLA_NOPE = 64
MLA_ROPE = 32
MLA_V = 64
MLA_Q_LORA = 384
MLA_KV_LORA = 256
MLA_WIDTH = MLA_HEADS * MLA_V
MLA_SCALE = (MLA_NOPE + MLA_ROPE) ** -0.5
DIFF_HEADS = 4
DIFF_HD = 64
DIFF_WIDTH = DIFF_HEADS * 2 * DIFF_HD
DIFF_SCALE = DIFF_HD ** -0.5
DEPTH = 1
ALPHA = (2 * DEPTH) ** 0.25
LAM_INIT = 0.8 - 0.6 * math.exp(-0.3 * 0)
IN_SPLITS = (MLA_Q_LORA, MLA_KV_LORA, MLA_ROPE, MLA_WIDTH, DIFF_WIDTH, DIFF_WIDTH, DIFF_WIDTH, DIFF_WIDTH,
             2 * D_MODEL)
IN_OFFSETS = tuple(int(o) for o in np.cumsum((0,) + IN_SPLITS))
LOG2E = math.log2(math.e)

V7X_LANES = 128
V7X_VMEM_LIMIT_BYTES = 56 * 2**20
MOD_ROWS = 16

PROJ_TOKENS = 256
KV_TILE = 256
Q_TILE = 512
OUT_TOKENS = 512

_NAT_CQ = (0, 384)
_NAT_CKV = (384, 640)
_NAT_DK = (640, 1152)
_NAT_DKSW = (1152, 1664)
_NAT_KR = (1664, 1792)
_NAT_KRSW = (1792, 1920)
_NAT_COLS = 1920
_T_DQ = (0, 512)
_T_DV = (512, 1024)
_T_GA = (1024, 1536)
_T_GD = (1536, 2048)
_T_MERGE = (2048, 4096)
_T_ROWS = 4096


def _nt_dot(a, b):
    return lax.dot_general(a, b, (((1,), (1,)), ((), ())), preferred_element_type=F32)


def _tn_dot(a, b):
    return lax.dot_general(a, b, (((0,), (0,)), ((), ())), preferred_element_type=F32)


def _dot(a, b):
    return jnp.dot(a, b, preferred_element_type=F32)


def _const_spec(shape):
    nd = len(shape)
    return pl.BlockSpec(shape, lambda *_: (0,) * nd)


def _params(semantics):
    return pltpu.CompilerParams(dimension_semantics=semantics, vmem_limit_bytes=V7X_VMEM_LIMIT_BYTES)


def _rope_tables(seq, width):
    h2 = width // 4
    pos = jnp.arange(seq, dtype=jnp.int32)
    row = (pos // GRID_W).astype(F32)
    col = (pos % GRID_W).astype(F32)
    inv = ROPE_THETA ** (-jnp.arange(h2, dtype=F32) / h2)

    def part(p):
        ang = p[:, None] * inv[None, :]
        c, s = jnp.cos(ang), jnp.sin(ang)
        return jnp.concatenate([c, c], -1), jnp.concatenate([-s, s], -1)

    cr, sr = part(row)
    cc, sc = part(col)
    return jnp.concatenate([cr, cc], -1), jnp.concatenate([sr, sc], -1)


def _swap_perm(width):
    half, h2 = width // 2, width // 4
    j = np.arange(width)
    return (j // half) * half + ((j % half) + h2) % half


def _swap_rows(x, width):
    h2 = width // 4
    p = [x[i * h2:(i + 1) * h2] for i in range(4)]
    return jnp.concatenate([p[1], p[0], p[3], p[2]], axis=0)


def _tables(seq, rotary):
    if rotary:
        c64, s64 = _rope_tables(seq, DIFF_HD)
        c32, s32 = _rope_tables(seq, MLA_ROPE)
    else:
        c64, s64 = jnp.ones((seq, DIFF_HD), F32), jnp.zeros((seq, DIFF_HD), F32)
        c32, s32 = jnp.ones((seq, MLA_ROPE), F32), jnp.zeros((seq, MLA_ROPE), F32)
    pad = jnp.zeros((seq, V7X_LANES - MLA_ROPE), F32)
    nat = jnp.concatenate([c64, c64, s64, s64, c32, pad, s32, pad], axis=-1)
    tr = jnp.concatenate([c32, s32, c64, s64], axis=-1).T
    return nat, tr


def _mod_kernel(c_ref, w_ref, b_ref, o_ref):
    c = c_ref[...]
    s = c * jax.nn.sigmoid(c)
    o_ref[...] = jnp.dot(s, w_ref[...], precision=lax.Precision.HIGHEST,
                         preferred_element_type=F32) + b_ref[...]


def _modulation(cc, w_mod, b_mod):
    rows, d = cc.shape
    n = w_mod.shape[1]
    return pl.pallas_call(
        _mod_kernel,
        grid=(n // d,),
        in_specs=[pl.BlockSpec((rows, d), lambda j: (0, 0)),
                  pl.BlockSpec((d, d), lambda j: (0, j)),
                  pl.BlockSpec((1, d), lambda j: (0, j))],
        out_specs=pl.BlockSpec((rows, d), lambda j: (0, j)),
        out_shape=jax.ShapeDtypeStruct((rows, n), F32),
        compiler_params=_params(("arbitrary",)),
        name="modulation",
    )(cc, w_mod, b_mod)


def _rms_rows(c, w):
    return c * lax.rsqrt(jnp.mean(c * c, axis=-1, keepdims=True) + RMS_EPS) * w


def _proj_kernel(*refs, latent, n_kv):
    if latent:
        (x_ref, scale_ref, shift_ref, wnat_ref, wt_ref, qn_ref, kvn_ref, wuq_ref, wuk_ref, ekr_ref, wuv_ref,
         tabn_ref, tabt_ref,
         ka_ref, vta_ref, kd_ref, vtd_ref, qta_ref, qtd_ref, gat_ref, gdt_ref, mt_ref) = refs
    else:
        (x_ref, scale_ref, shift_ref, wnat_ref, wt_ref, kvn_ref, wuk_ref, ekr_ref, wuv_ref, tabn_ref,
         ka_ref, vta_ref, kd_ref, vtd_ref) = refs
    tk = vta_ref.shape[-1]
    lanes = V7X_LANES

    h = (x_ref[0] * (1.0 + scale_ref[0]) + shift_ref[0]).astype(BF16)
    pn = _dot(h, wnat_ref[...])

    ckv = _rms_rows(pn[:, _NAT_CKV[0]:_NAT_CKV[1]], kvn_ref[...]).astype(BF16)
    kr = (pn[:, _NAT_KR[0]:_NAT_KR[1]] * tabn_ref[:, 2 * lanes:3 * lanes]
          + pn[:, _NAT_KRSW[0]:_NAT_KRSW[1]] * tabn_ref[:, 3 * lanes:4 * lanes])
    ka = _dot(ckv, wuk_ref[...]) + _dot(kr.astype(BF16), ekr_ref[...])
    ka_ref[0] = ka.astype(BF16)
    vta = _nt_dot(wuv_ref[...], ckv).astype(BF16)
    for j in range(n_kv):
        vta_ref[0, j] = vta[:, j * tk:(j + 1) * tk]

    for hh in range(DIFF_HEADS):
        a = _NAT_DK[0] + hh * lanes
        b = _NAT_DKSW[0] + hh * lanes
        kd = pn[:, a:a + lanes] * tabn_ref[:, 0:lanes] + pn[:, b:b + lanes] * tabn_ref[:, lanes:2 * lanes]
        kd_ref[0, :, hh * lanes:(hh + 1) * lanes] = kd.astype(BF16)

    dv_rows = _T_DV if latent else (0, DIFF_WIDTH)
    vtd = _nt_dot(wt_ref[dv_rows[0]:dv_rows[1], :], h).astype(BF16)
    for j in range(n_kv):
        vtd_ref[0, j] = vtd[:, j * tk:(j + 1) * tk]

    if not latent:
        return

    cq = _rms_rows(pn[:, _NAT_CQ[0]:_NAT_CQ[1]], qn_ref[...]).astype(BF16)
    qta = _nt_dot(wuq_ref[...], cq) * (MLA_SCALE * LOG2E)
    c32, s32 = tabt_ref[0:MLA_ROPE, :], tabt_ref[MLA_ROPE:2 * MLA_ROPE, :]
    for hh in range(MLA_HEADS):
        base = hh * lanes
        r0, r1 = base + MLA_NOPE, base + MLA_NOPE + MLA_ROPE
        qta_ref[0, base:r0, :] = qta[base:r0].astype(BF16)
        r = qta[r0:r1]
        qta_ref[0, r0:r1, :] = (r * c32 + _swap_rows(r, MLA_ROPE) * s32).astype(BF16)
        qta_ref[0, r1:base + lanes, :] = qta[r1:base + lanes].astype(BF16)

    qtd = _nt_dot(wt_ref[_T_DQ[0]:_T_DQ[1], :], h) * (DIFF_SCALE * LOG2E)
    c64 = tabt_ref[2 * MLA_ROPE:2 * MLA_ROPE + DIFF_HD, :]
    s64 = tabt_ref[2 * MLA_ROPE + DIFF_HD:2 * MLA_ROPE + 2 * DIFF_HD, :]
    for g in range(2 * DIFF_HEADS):
        blk = qtd[g * DIFF_HD:(g + 1) * DIFF_HD]
        qtd_ref[0, g * DIFF_HD:(g + 1) * DIFF_HD, :] = (blk * c64 + _swap_rows(blk, DIFF_HD) * s64).astype(BF16)

    ga = _nt_dot(wt_ref[_T_GA[0]:_T_GA[1], :], h)
    gat_ref[0] = (ga * jax.nn.sigmoid(ga)).astype(BF16)
    gd = _nt_dot(wt_ref[_T_GD[0]:_T_GD[1], :], h)
    gdt_ref[0] = (gd * jax.nn.sigmoid(gd)).astype(BF16)
    mt_ref[0] = jax.nn.sigmoid(_nt_dot(wt_ref[_T_MERGE[0]:_T_MERGE[1], :], h)).astype(BF16)


def _projection(tokens, scale, shift, w, tabn, tabt, *, latent):
    bsz, t, d = tokens.shape
    ts = min(PROJ_TOKENS, t)
    tk = min(KV_TILE, t)
    n_kv = ts // tk
    assert t % ts == 0 and ts % tk == 0
    per_batch = scale.shape[0] == bsz
    mod_map = (lambda b, i: (b, 0, 0)) if per_batch else (lambda b, i: (0, 0, 0))
    wt = w["wt"] if latent else w["wt_ctx"]

    in_arrays = [tokens, scale, shift, w["wnat"], wt]
    in_specs = [pl.BlockSpec((1, ts, d), lambda b, i: (b, i, 0)),
                pl.BlockSpec((1, 1, d), mod_map), pl.BlockSpec((1, 1, d), mod_map),
                _const_spec(w["wnat"].shape), _const_spec(wt.shape)]
    if latent:
        in_arrays.append(w["qn"])
        in_specs.append(_const_spec(w["qn"].shape))
    in_arrays.append(w["kvn"])
    in_specs.append(_const_spec(w["kvn"].shape))
    if latent:
        in_arrays.append(w["wuq"])
        in_specs.append(_const_spec(w["wuq"].shape))
    for name in ("wuk", "ekr", "wuv"):
        in_arrays.append(w[name])
        in_specs.append(_const_spec(w[name].shape))
    in_arrays.append(tabn)
    in_specs.append(pl.BlockSpec((ts, tabn.shape[1]), lambda b, i: (i, 0)))
    if latent:
        in_arrays.append(tabt)
        in_specs.append(pl.BlockSpec((tabt.shape[0], ts), lambda b, i: (0, i)))

    hp = MLA_HEADS * V7X_LANES
    out_shape = [jax.ShapeDtypeStruct((bsz, t, hp), BF16),
                 jax.ShapeDtypeStruct((bsz, t // tk, MLA_WIDTH, tk), BF16),
                 jax.ShapeDtypeStruct((bsz, t, DIFF_WIDTH), BF16),
                 jax.ShapeDtypeStruct((bsz, t // tk, DIFF_WIDTH, tk), BF16)]
    out_specs = [pl.BlockSpec((1, ts, hp), lambda b, i: (b, i, 0)),
                 pl.BlockSpec((1, n_kv, MLA_WIDTH, tk), lambda b, i: (b, i, 0, 0)),
                 pl.BlockSpec((1, ts, DIFF_WIDTH), lambda b, i: (b, i, 0)),
                 pl.BlockSpec((1, n_kv, DIFF_WIDTH, tk), lambda b, i: (b, i, 0, 0))]
    if latent:
        for rows in (hp, DIFF_WIDTH, MLA_WIDTH, DIFF_WIDTH, 2 * D_MODEL):
            out_shape.append(jax.ShapeDtypeStruct((bsz, rows, t), BF16))
            out_specs.append(pl.BlockSpec((1, rows, ts), lambda b, i: (b, 0, i)))

    return pl.pallas_call(
        functools.partial(_proj_kernel, latent=latent, n_kv=n_kv),
        grid=(bsz, t // ts),
        in_specs=in_specs,
        out_specs=out_specs,
        out_shape=out_shape,
        compiler_params=_params(("parallel", "parallel")),
        name="projection_latent" if latent else "projection_context",
    )(*in_arrays)


def _softmax_step(k, q, vt, m_sc, l_sc, acc_sc):
    s = _dot(k, q)
    m_old = m_sc[...]
    m_new = jnp.maximum(m_old, jnp.max(s, axis=0, keepdims=True))
    alpha = jnp.exp2(m_old - m_new)
    p = jnp.exp2(s - m_new)
    l_sc[...] = alpha * l_sc[...] + jnp.sum(p, axis=0, keepdims=True)
    acc_sc[...] = alpha * acc_sc[...] + _dot(vt, p.astype(BF16))
    m_sc[...] = m_new


def _init_stats(m_sc, l_sc, acc_sc):
    m_sc[...] = jnp.full(m_sc.shape, -jnp.inf, F32)
    l_sc[...] = jnp.zeros(l_sc.shape, F32)
    acc_sc[...] = jnp.zeros(acc_sc.shape, F32)


def _mla_attn_kernel(q_ref, kc_ref, kl_ref, vc_ref, vl_ref, o_ref, m_sc, l_sc, acc_sc):
    tk = vl_ref.shape[-1]
    q = q_ref[0]
    _init_stats(m_sc, l_sc, acc_sc)
    for j in range(vc_ref.shape[1]):
        _softmax_step(kc_ref[0, j * tk:(j + 1) * tk, :], q, vc_ref[0, j], m_sc, l_sc, acc_sc)

    def body(j, carry):
        off = pl.multiple_of(j * tk, tk)
        _softmax_step(kl_ref[0, pl.ds(off, tk), :], q, vl_ref[0, j], m_sc, l_sc, acc_sc)
        return carry

    lax.fori_loop(0, vl_ref.shape[1], body, 0)
    o_ref[0] = (acc_sc[...] / l_sc[...]).astype(o_ref.dtype)


def _diff_attn_kernel(q_ref, kc_ref, kl_ref, vc_ref, vl_ref, lam_a_ref, lam_b_ref, subln_ref, o_ref,
                      m1, l1, a1, m2, l2, a2):
    tk = vl_ref.shape[-1]
    q = q_ref[0]
    zeros = jnp.zeros((DIFF_HD, q.shape[1]), q.dtype)
    q1 = jnp.concatenate([q[:DIFF_HD], zeros], axis=0)
    q2 = jnp.concatenate([zeros, q[DIFF_HD:]], axis=0)
    _init_stats(m1, l1, a1)
    _init_stats(m2, l2, a2)

    def step(k, vt):
        _softmax_step(k, q1, vt, m1, l1, a1)
        _softmax_step(k, q2, vt, m2, l2, a2)

    for j in range(vc_ref.shape[1]):
        step(kc_ref[0, j * tk:(j + 1) * tk, :], vc_ref[0, j])

    def body(j, carry):
        off = pl.multiple_of(j * tk, tk)
        step(kl_ref[0, pl.ds(off, tk), :], vl_ref[0, j])
        return carry

    lax.fori_loop(0, vl_ref.shape[1], body, 0)

    e = jnp.exp(jnp.sum(lam_a_ref[...] * lam_b_ref[...], axis=1, keepdims=True))
    sign = jnp.where(lax.broadcasted_iota(jnp.int32, e.shape, 0) == 0, 1.0, -1.0)
    lam = jnp.sum(e * sign, axis=0, keepdims=True) + LAM_INIT
    o = a1[...] / l1[...] - lam * (a2[...] / l2[...])
    o = o * lax.rsqrt(jnp.mean(o * o, axis=0, keepdims=True) + SUBLN_EPS)
    o_ref[0] = ((o * subln_ref[...]) * (1.0 - LAM_INIT)).astype(o_ref.dtype)


def _attention(qt, k_ctx, k_lat, vt_ctx, vt_lat, *, heads, dv, extra=(), kernel_fn, n_stats, name):
    bsz, _, s = qt.shape
    c = k_ctx.shape[1]
    tq = min(Q_TILE, s)
    tk = vt_lat.shape[-1]
    lanes = V7X_LANES
    assert s % tq == 0
    in_specs = [pl.BlockSpec((1, lanes, tq), lambda b, h, i: (b, h, i)),
                pl.BlockSpec((1, c, lanes), lambda b, h, i: (b, 0, h)),
                pl.BlockSpec((1, s, lanes), lambda b, h, i: (b, 0, h)),
                pl.BlockSpec((1, c // tk, dv, tk), lambda b, h, i: (b, 0, h, 0)),
                pl.BlockSpec((1, s // tk, dv, tk), lambda b, h, i: (b, 0, h, 0))]
    in_specs += [_const_spec(a.shape) for a in extra]
    scratch = []
    for _ in range(n_stats):
        scratch += [pltpu.VMEM((1, tq), F32), pltpu.VMEM((1, tq), F32), pltpu.VMEM((dv, tq), F32)]
    return pl.pallas_call(
        kernel_fn,
        grid=(bsz, heads, s // tq),
        in_specs=in_specs,
        out_specs=pl.BlockSpec((1, dv, tq), lambda b, h, i: (b, h, i)),
        out_shape=jax.ShapeDtypeStruct((bsz, heads * dv, s), BF16),
        scratch_shapes=scratch,
        compiler_params=_params(("parallel", "parallel", "arbitrary")),
        name=name,
    )(qt, k_ctx, k_lat, vt_ctx, vt_lat, *extra)


def _out_kernel(oa_ref, od_ref, ga_ref, gd_ref, m_ref, x_ref, gate_ref, woa_ref, wob_ref, wout_ref,
                g_ref, b_ref, o_ref):
    ua = (oa_ref[0].astype(F32) * ga_ref[0].astype(F32)).astype(BF16)
    ud = (od_ref[0].astype(F32) * gd_ref[0].astype(F32)).astype(BF16)
    ya = _dot(woa_ref[...], ua)
    yd = _dot(wob_ref[...], ud)
    mix = m_ref[0, :D_MODEL, :].astype(F32) * ya + m_ref[0, D_MODEL:, :].astype(F32) * yd
    y = _tn_dot(mix.astype(BF16), wout_ref[...])
    z = ALPHA * x_ref[0] + gate_ref[0] * y
    mu = jnp.mean(z, axis=-1, keepdims=True)
    zc = z - mu
    var = jnp.mean(zc * zc, axis=-1, keepdims=True)
    o_ref[0] = zc * lax.rsqrt(var + LN_EPS) * g_ref[...] + b_ref[...]


def _output(oa, od, ga, gd, mt, x, gate, w):
    bsz, s, d = x.shape
    tf = min(OUT_TOKENS, s)
    assert s % tf == 0
    feat = lambda rows: pl.BlockSpec((1, rows, tf), lambda b, i: (b, 0, i))
    return pl.pallas_call(
        _out_kernel,
        grid=(bsz, s // tf),
        in_specs=[feat(MLA_WIDTH), feat(DIFF_WIDTH), feat(MLA_WIDTH), feat(DIFF_WIDTH), feat(2 * d),
                  pl.BlockSpec((1, tf, d), lambda b, i: (b, i, 0)),
                  pl.BlockSpec((1, 1, d), lambda b, i: (b, 0, 0)),
                  _const_spec(w["woa"].shape), _const_spec(w["wob"].shape), _const_spec(w["wout"].shape),
                  _const_spec(w["ln_g"].shape), _const_spec(w["ln_b"].shape)],
        out_specs=pl.BlockSpec((1, tf, d), lambda b, i: (b, i, 0)),
        out_shape=jax.ShapeDtypeStruct((bsz, s, d), F32),
        compiler_params=_params(("parallel", "parallel")),
        name="output",
    )(oa, od, ga, gd, mt, x, gate, w["woa"], w["wob"], w["wout"], w["ln_g"], w["ln_b"])


def _prep_weights(w_in, mla_q_norm, mla_kv_norm, w_uq, w_ukv, w_oa, w_ob, w_out, ln_g, ln_b):
    o = IN_OFFSETS
    cols = lambda i: w_in[:, o[i]:o[i + 1]]
    w_cq, w_ckv, w_kr, w_ga, w_dq, w_dk, w_dv, w_gd, w_mg = (cols(i) for i in range(9))
    lanes = V7X_LANES
    pad_kr = jnp.zeros((D_MODEL, lanes - MLA_ROPE), F32)
    dk_perm = np.concatenate([g * DIFF_HD + _swap_perm(DIFF_HD) for g in range(2 * DIFF_HEADS)])
    wnat = jnp.concatenate([w_cq, w_ckv, w_dk, w_dk[:, dk_perm], w_kr, pad_kr,
                            w_kr[:, _swap_perm(MLA_ROPE)], pad_kr], axis=1)
    assert wnat.shape[1] == _NAT_COLS
    wt = jnp.concatenate([w_dq, w_dv, w_ga, w_gd, w_mg], axis=1).T
    assert wt.shape[0] == _T_ROWS

    uq = w_uq.reshape(MLA_Q_LORA, MLA_HEADS, MLA_NOPE + MLA_ROPE)
    uq = jnp.pad(uq, ((0, 0), (0, 0), (0, lanes - MLA_NOPE - MLA_ROPE)))
    wuq = uq.reshape(MLA_Q_LORA, MLA_HEADS * lanes).T
    ukv = w_ukv.reshape(MLA_KV_LORA, MLA_HEADS, MLA_NOPE + MLA_V)
    wuk = jnp.pad(ukv[:, :, :MLA_NOPE], ((0, 0), (0, 0), (0, lanes - MLA_NOPE)))
    wuk = wuk.reshape(MLA_KV_LORA, MLA_HEADS * lanes)
    wuv = ukv[:, :, MLA_NOPE:].reshape(MLA_KV_LORA, MLA_WIDTH).T
    ekr = np.zeros((lanes, MLA_HEADS * lanes), np.float32)
    for hh in range(MLA_HEADS):
        ekr[np.arange(MLA_ROPE), hh * lanes + MLA_NOPE + np.arange(MLA_ROPE)] = 1.0
    return {
        "wnat": wnat.astype(BF16), "wt": wt.astype(BF16), "wt_ctx": w_dv.T.astype(BF16),
        "qn": mla_q_norm[None, :], "kvn": mla_kv_norm[None, :],
        "wuq": wuq.astype(BF16), "wuk": wuk.astype(BF16), "wuv": wuv.astype(BF16),
        "ekr": jnp.asarray(ekr, BF16),
        "woa": w_oa.T.astype(BF16), "wob": w_ob.T.astype(BF16), "wout": w_out.astype(BF16),
        "ln_g": ln_g[None, :], "ln_b": ln_b[None, :],
    }


def kernel(x, c, ctx, c_ctx, w_mod, b_mod, w_in, mla_q_norm, mla_kv_norm, w_uq, w_ukv, diff_lambda,
           diff_subln, w_oa, w_ob, w_out, ln_g, ln_b):
    bsz, s, d = x.shape
    n_ctx = ctx.shape[1]
    assert w_mod.shape[0] == DEPTH == 1 and d == D_MODEL and s % GRID_W == 0 and bsz + 1 <= MOD_ROWS

    cc = jnp.concatenate([c, c_ctx[None, :], jnp.zeros((MOD_ROWS - bsz - 1, d), F32)], axis=0)
    mod = _modulation(cc, w_mod[0], b_mod[0][None, :])
    shift, scale, gate = (mod[:bsz, i * d:(i + 1) * d][:, None, :] for i in range(3))
    shift_c, scale_c = (mod[bsz:bsz + 1, i * d:(i + 1) * d][:, None, :] for i in range(2))

    w = _prep_weights(w_in[0], mla_q_norm[0], mla_kv_norm[0], w_uq[0], w_ukv[0], w_oa[0], w_ob[0], w_out[0],
                      ln_g[0], ln_b[0])
    tabn, tabt = _tables(s, rotary=True)
    tabn_c, _ = _tables(n_ctx, rotary=False)

    ka_c, vta_c, kd_c, vtd_c = _projection(ctx, scale_c, shift_c, w, tabn_c, None, latent=False)
    ka, vta, kd, vtd, qta, qtd, gat, gdt, mt = _projection(x, scale, shift, w, tabn, tabt, latent=True)

    oa = _attention(qta, ka_c, ka, vta_c, vta, heads=MLA_HEADS, dv=MLA_V, kernel_fn=_mla_attn_kernel,
                    n_stats=1, name="attention_mla")
    lam_a = diff_lambda[0][0::2]
    lam_b = diff_lambda[0][1::2]
    od = _attention(qtd, kd_c, kd, vtd_c, vtd, heads=DIFF_HEADS, dv=2 * DIFF_HD,
                    extra=(lam_a, lam_b, diff_subln[0][:, None]), kernel_fn=_diff_attn_kernel,
                    n_stats=2, name="attention_diff")
    return _output(oa, od, gat, gdt, mt, x, gate, w)
```

```python
import functools
import math

import numpy as np
import jax
import jax.numpy as jnp
from jax import lax
from jax.experimental import pallas as pl
from jax.experimental.pallas import tpu as pltpu

F32 = jnp.float32
BF16 = jnp.bfloat16

D_MODEL = 1024
GRID_W = 64
ROPE_THETA = 10000.0
LN_EPS = 1e-5
RMS_EPS = 1e-6
SUBLN_EPS = 1e-5
MLA_HEADS = 8
MLA_NOPE = 64
MLA_ROPE = 32
MLA_V = 64
MLA_Q_LORA = 384
MLA_KV_LORA = 256
MLA_WIDTH = MLA_HEADS * MLA_V
MLA_SCALE = (MLA_NOPE + MLA_ROPE) ** -0.5
DIFF_HEADS = 4
DIFF_HD = 64
DIFF_WIDTH = DIFF_HEADS * 2 * DIFF_HD
DIFF_SCALE = DIFF_HD ** -0.5
DEPTH = 1
ALPHA = (2 * DEPTH) ** 0.25
LAM_INIT = 0.8 - 0.6 * math.exp(-0.3 * 0)
IN_SPLITS = (MLA_Q_LORA, MLA_KV_LORA, MLA_ROPE, MLA_WIDTH, DIFF_WIDTH, DIFF_WIDTH, DIFF_WIDTH, DIFF_WIDTH,
             2 * D_MODEL)
IN_OFFSETS = tuple(int(o) for o in np.cumsum((0,) + IN_SPLITS))
LOG2E = math.log2(math.e)

V7X_LANES = 128
V7X_VMEM_LIMIT_BYTES = 56 * 2**20
MOD_ROWS = 16

PROJ_TOKENS = 256
KV_TILE = 256
Q_TILE = 512
OUT_TOKENS = 512

_NAT_CQ = (0, 384)
_NAT_CKV = (384, 640)
_NAT_DK = (640, 1152)
_NAT_DKSW = (1152, 1664)
_NAT_KR = (1664, 1792)
_NAT_KRSW = (1792, 1920)
_NAT_COLS = 1920
_T_DQ = (0, 512)
_T_DV = (512, 1024)
_T_GA = (1024, 1536)
_T_GD = (1536, 2048)
_T_MERGE = (2048, 4096)
_T_ROWS = 4096


def _nt_dot(a, b):
    return lax.dot_general(a, b, (((1,), (1,)), ((), ())), preferred_element_type=F32)


def _tn_dot(a, b):
    return lax.dot_general(a, b, (((0,), (0,)), ((), ())), preferred_element_type=F32)


def _dot(a, b):
    return jnp.dot(a, b, preferred_element_type=F32)


def _const_spec(shape):
    nd = len(shape)
    return pl.BlockSpec(shape, lambda *_: (0,) * nd)


def _params(semantics):
    return pltpu.CompilerParams(dimension_semantics=semantics, vmem_limit_bytes=V7X_VMEM_LIMIT_BYTES)


def _rope_tables(seq, width):
    h2 = width // 4
    pos = jnp.arange(seq, dtype=jnp.int32)
    row = (pos // GRID_W).astype(F32)
    col = (pos % GRID_W).astype(F32)
    inv = ROPE_THETA ** (-jnp.arange(h2, dtype=F32) / h2)

    def part(p):
        ang = p[:, None] * inv[None, :]
        c, s = jnp.cos(ang), jnp.sin(ang)
        return jnp.concatenate([c, c], -1), jnp.concatenate([-s, s], -1)

    cr, sr = part(row)
    cc, sc = part(col)
    return jnp.concatenate([cr, cc], -1), jnp.concatenate([sr, sc], -1)


def _swap_perm(width):
    half, h2 = width // 2, width // 4
    j = np.arange(width)
    return (j // half) * half + ((j % half) + h2) % half


def _swap_rows(x, width):
    h2 = width // 4
    p = [x[i * h2:(i + 1) * h2] for i in range(4)]
    return jnp.concatenate([p[1], p[0], p[3], p[2]], axis=0)


def _tables(seq, rotary):
    if rotary:
        c64, s64 = _rope_tables(seq, DIFF_HD)
        c32, s32 = _rope_tables(seq, MLA_ROPE)
    else:
        c64, s64 = jnp.ones((seq, DIFF_HD), F32), jnp.zeros((seq, DIFF_HD), F32)
        c32, s32 = jnp.ones((seq, MLA_ROPE), F32), jnp.zeros((seq, MLA_ROPE), F32)
    pad = jnp.zeros((seq, V7X_LANES - MLA_ROPE), F32)
    nat = jnp.concatenate([c64, c64, s64, s64, c32, pad, s32, pad], axis=-1)
    tr = jnp.concatenate([c32, s32, c64, s64], axis=-1).T
    return nat, tr


def _mod_kernel(c_ref, w_ref, b_ref, o_ref):
    c = c_ref[...]
    s = c * jax.nn.sigmoid(c)
    o_ref[...] = jnp.dot(s, w_ref[...], precision=lax.Precision.HIGHEST,
                         preferred_element_type=F32) + b_ref[...]


def _modulation(cc, w_mod, b_mod):
    rows, d = cc.shape
    n = w_mod.shape[1]
    return pl.pallas_call(
        _mod_kernel,
        grid=(n // d,),
        in_specs=[pl.BlockSpec((rows, d), lambda j: (0, 0)),
                  pl.BlockSpec((d, d), lambda j: (0, j)),
                  pl.BlockSpec((1, d), lambda j: (0, j))],
        out_specs=pl.BlockSpec((rows, d), lambda j: (0, j)),
        out_shape=jax.ShapeDtypeStruct((rows, n), F32),
        compiler_params=_params(("arbitrary",)),
        name="modulation",
    )(cc, w_mod, b_mod)


def _rms_rows(c, w):
    return c * lax.rsqrt(jnp.mean(c * c, axis=-1, keepdims=True) + RMS_EPS) * w


def _proj_kernel(*refs, latent, n_kv):
    if latent:
        (x_ref, scale_ref, shift_ref, wnat_ref, wt_ref, qn_ref, kvn_ref, wuq_ref, wuk_ref, ekr_ref, wuv_ref,
         tabn_ref, tabt_ref, _, _, _, _,
         ka_ref, vta_ref, kd_ref, vtd_ref, qta_ref, qtd_ref, gat_ref, gdt_ref, mt_ref) = refs
    else:
        (x_ref, scale_ref, shift_ref, wnat_ref, wt_ref, kvn_ref, wuk_ref, ekr_ref, wuv_ref, tabn_ref,
         ka_ref, vta_ref, kd_ref, vtd_ref) = refs
    tk = vta_ref.shape[-1]
    lanes = V7X_LANES

    h = (x_ref[0] * (1.0 + scale_ref[0]) + shift_ref[0]).astype(BF16)
    pn = _dot(h, wnat_ref[...])

    ckv = _rms_rows(pn[:, _NAT_CKV[0]:_NAT_CKV[1]], kvn_ref[...]).astype(BF16)
    kr = (pn[:, _NAT_KR[0]:_NAT_KR[1]] * tabn_ref[:, 2 * lanes:3 * lanes]
          + pn[:, _NAT_KRSW[0]:_NAT_KRSW[1]] * tabn_ref[:, 3 * lanes:4 * lanes])
    ka = _dot(ckv, wuk_ref[...]) + _dot(kr.astype(BF16), ekr_ref[...])
    ka_ref[0] = ka.astype(BF16)
    vta = _nt_dot(wuv_ref[...], ckv).astype(BF16)
    for j in range(n_kv):
        vta_ref[0, j] = vta[:, j * tk:(j + 1) * tk]

    for hh in range(DIFF_HEADS):
        a = _NAT_DK[0] + hh * lanes
        b = _NAT_DKSW[0] + hh * lanes
        kd = pn[:, a:a + lanes] * tabn_ref[:, 0:lanes] + pn[:, b:b + lanes] * tabn_ref[:, lanes:2 * lanes]
        kd_ref[0, :, hh * lanes:(hh + 1) * lanes] = kd.astype(BF16)

    dv_rows = _T_DV if latent else (0, DIFF_WIDTH)
    vtd = _nt_dot(wt_ref[dv_rows[0]:dv_rows[1], :], h).astype(BF16)
    for j in range(n_kv):
        vtd_ref[0, j] = vtd[:, j * tk:(j + 1) * tk]

    if not latent:
        return

    cq = _rms_rows(pn[:, _NAT_CQ[0]:_NAT_CQ[1]], qn_ref[...]).astype(BF16)
    qta = _nt_dot(wuq_ref[...], cq) * (MLA_SCALE * LOG2E)
    c32, s32 = tabt_ref[0:MLA_ROPE, :], tabt_ref[MLA_ROPE:2 * MLA_ROPE, :]
    for hh in range(MLA_HEADS):
        base = hh * lanes
        r0, r1 = base + MLA_NOPE, base + MLA_NOPE + MLA_ROPE
        qta_ref[0, base:r0, :] = qta[base:r0].astype(BF16)
        r = qta[r0:r1]
        qta_ref[0, r0:r1, :] = (r * c32 + _swap_rows(r, MLA_ROPE) * s32).astype(BF16)
        qta_ref[0, r1:base + lanes, :] = qta[r1:base + lanes].astype(BF16)

    qtd = _nt_dot(wt_ref[_T_DQ[0]:_T_DQ[1], :], h) * (DIFF_SCALE * LOG2E)
    c64 = tabt_ref[2 * MLA_ROPE:2 * MLA_ROPE + DIFF_HD, :]
    s64 = tabt_ref[2 * MLA_ROPE + DIFF_HD:2 * MLA_ROPE + 2 * DIFF_HD, :]
    for g in range(2 * DIFF_HEADS):
        blk = qtd[g * DIFF_HD:(g + 1) * DIFF_HD]
        qtd_ref[0, g * DIFF_HD:(g + 1) * DIFF_HD, :] = (blk * c64 + _swap_rows(blk, DIFF_HD) * s64).astype(BF16)

    ga = _nt_dot(wt_ref[_T_GA[0]:_T_GA[1], :], h)
    gat_ref[0] = (ga * jax.nn.sigmoid(ga)).astype(BF16)
    gd = _nt_dot(wt_ref[_T_GD[0]:_T_GD[1], :], h)
    gdt_ref[0] = (gd * jax.nn.sigmoid(gd)).astype(BF16)
    mt_ref[0] = jax.nn.sigmoid(_nt_dot(wt_ref[_T_MERGE[0]:_T_MERGE[1], :], h)).astype(BF16)


def _projection(tokens, scale, shift, w, tabn, tabt, *, n_total, tile_offset, shared=None):
    latent = shared is not None
    bsz, t, d = tokens.shape
    ts = min(PROJ_TOKENS, t)
    tk = KV_TILE
    n_kv = ts // tk
    assert t % ts == 0 and ts % tk == 0 and n_total % tk == 0
    off = tile_offset
    per_batch = scale.shape[0] == bsz
    mod_map = (lambda b, i: (b, 0, 0)) if per_batch else (lambda b, i: (0, 0, 0))
    wt = w["wt"] if latent else w["wt_ctx"]

    in_arrays = [tokens, scale, shift, w["wnat"], wt]
    in_specs = [pl.BlockSpec((1, ts, d), lambda b, i: (b, i, 0)),
                pl.BlockSpec((1, 1, d), mod_map), pl.BlockSpec((1, 1, d), mod_map),
                _const_spec(w["wnat"].shape), _const_spec(wt.shape)]
    names = ("qn", "kvn", "wuq", "wuk", "ekr", "wuv") if latent else ("kvn", "wuk", "ekr", "wuv")
    for name in names:
        in_arrays.append(w[name])
        in_specs.append(_const_spec(w[name].shape))
    in_arrays.append(tabn)
    in_specs.append(pl.BlockSpec((ts, tabn.shape[1]), lambda b, i: (i, 0)))
    aliases = {}
    if latent:
        in_arrays.append(tabt)
        in_specs.append(pl.BlockSpec((tabt.shape[0], ts), lambda b, i: (0, i)))
        for j, buf in enumerate(shared):
            aliases[len(in_arrays)] = j
            in_arrays.append(buf)
            in_specs.append(pl.BlockSpec(memory_space=pl.ANY))

    hp = MLA_HEADS * V7X_LANES
    out_shape = [jax.ShapeDtypeStruct((bsz, n_total, hp), BF16),
                 jax.ShapeDtypeStruct((bsz, n_total // tk, MLA_WIDTH, tk), BF16),
                 jax.ShapeDtypeStruct((bsz, n_total, DIFF_WIDTH), BF16),
                 jax.ShapeDtypeStruct((bsz, n_total // tk, DIFF_WIDTH, tk), BF16)]
    out_specs = [pl.BlockSpec((1, ts, hp), lambda b, i: (b, i + off, 0)),
                 pl.BlockSpec((1, n_kv, MLA_WIDTH, tk), lambda b, i: (b, i + off, 0, 0)),
                 pl.BlockSpec((1, ts, DIFF_WIDTH), lambda b, i: (b, i + off, 0)),
                 pl.BlockSpec((1, n_kv, DIFF_WIDTH, tk), lambda b, i: (b, i + off, 0, 0))]
    if latent:
        for rows in (hp, DIFF_WIDTH, MLA_WIDTH, DIFF_WIDTH, 2 * D_MODEL):
            out_shape.append(jax.ShapeDtypeStruct((bsz, rows, t), BF16))
            out_specs.append(pl.BlockSpec((1, rows, ts), lambda b, i: (b, 0, i)))

    return pl.pallas_call(
        functools.partial(_proj_kernel, latent=latent, n_kv=n_kv),
        grid=(bsz, t // ts),
        in_specs=in_specs,
        out_specs=out_specs,
        out_shape=out_shape,
        input_output_aliases=aliases,
        compiler_params=_params(("parallel", "parallel")),
        name="projection_latent" if latent else "projection_context",
    )(*in_arrays)


def _softmax_step(s, vt, m_sc, l_sc, acc_sc):
    m_old = m_sc[...]
    m_new = jnp.maximum(m_old, jnp.max(s, axis=0, keepdims=True))
    alpha = jnp.exp2(m_old - m_new)
    p = jnp.exp2(s - m_new)
    l_sc[...] = alpha * l_sc[...] + jnp.sum(p, axis=0, keepdims=True)
    acc_sc[...] = alpha * acc_sc[...] + _dot(vt, p.astype(BF16))
    m_sc[...] = m_new


def _init_stats(m_sc, l_sc, acc_sc):
    m_sc[...] = jnp.full(m_sc.shape, -jnp.inf, F32)
    l_sc[...] = jnp.zeros(l_sc.shape, F32)
    acc_sc[...] = jnp.zeros(acc_sc.shape, F32)


def _kv_pipeline(n_tiles, scores, process):
    scores(0, 0)
    n_pairs = (n_tiles - 1) // 2

    def body(t, carry):
        scores(2 * t + 1, 1)
        process(2 * t, 0)
        scores(2 * t + 2, 0)
        process(2 * t + 1, 1)
        return carry

    lax.fori_loop(0, n_pairs, body, 0)
    last = 2 * n_pairs
    if n_tiles % 2 == 0:
        scores(last + 1, 1)
        process(last, 0)
        process(last + 1, 1)
    else:
        process(last, 0)


def _kv_rows(k_ref, j, tk):
    off = j * tk if isinstance(j, int) else pl.multiple_of(j * tk, tk)
    return k_ref[0, pl.ds(off, tk), :]


def _mla_attn_kernel(q_ref, k_ref, vt_ref, o_ref, s_sc, m_sc, l_sc, acc_sc):
    n_tiles, tk = vt_ref.shape[1], vt_ref.shape[-1]
    q = q_ref[0]
    _init_stats(m_sc, l_sc, acc_sc)

    def scores(j, slot):
        s_sc[slot] = _dot(_kv_rows(k_ref, j, tk), q)

    def process(j, slot):
        _softmax_step(s_sc[slot], vt_ref[0, j], m_sc, l_sc, acc_sc)

    _kv_pipeline(n_tiles, scores, process)
    o_ref[0] = (acc_sc[...] / l_sc[...]).astype(o_ref.dtype)


def _diff_attn_kernel(q_ref, k_ref, vt_ref, lam_a_ref, lam_b_ref, subln_ref, o_ref,
                      s_sc, m1, l1, a1, m2, l2, a2):
    n_tiles, tk = vt_ref.shape[1], vt_ref.shape[-1]
    q = q_ref[0]
    zeros = jnp.zeros((DIFF_HD, q.shape[1]), q.dtype)
    q1 = jnp.concatenate([q[:DIFF_HD], zeros], axis=0)
    q2 = jnp.concatenate([zeros, q[DIFF_HD:]], axis=0)
    _init_stats(m1, l1, a1)
    _init_stats(m2, l2, a2)

    def scores(j, slot):
        k = _kv_rows(k_ref, j, tk)
        s_sc[slot, 0] = _dot(k, q1)
        s_sc[slot, 1] = _dot(k, q2)

    def process(j, slot):
        vt = vt_ref[0, j]
        _softmax_step(s_sc[slot, 0], vt, m1, l1, a1)
        _softmax_step(s_sc[slot, 1], vt, m2, l2, a2)

    _kv_pipeline(n_tiles, scores, process)

    e = jnp.exp(jnp.sum(lam_a_ref[...] * lam_b_ref[...], axis=1, keepdims=True))
    sign = jnp.where(lax.broadcasted_iota(jnp.int32, e.shape, 0) == 0, 1.0, -1.0)
    lam = jnp.sum(e * sign, axis=0, keepdims=True) + LAM_INIT
    o = a1[...] / l1[...] - lam * (a2[...] / l2[...])
    o = o * lax.rsqrt(jnp.mean(o * o, axis=0, keepdims=True) + SUBLN_EPS)
    o_ref[0] = ((o * subln_ref[...]) * (1.0 - LAM_INIT)).astype(o_ref.dtype)


def _attention(qt, k, vt, *, heads, dv, extra=(), kernel_fn, n_softmax, name):
    bsz, _, s = qt.shape
    t = k.shape[1]
    tq = min(Q_TILE, s)
    n_tiles, tk = vt.shape[1], vt.shape[-1]
    lanes = V7X_LANES
    assert s % tq == 0 and n_tiles * tk == t
    in_specs = [pl.BlockSpec((1, lanes, tq), lambda b, h, i: (b, h, i)),
                pl.BlockSpec((1, t, lanes), lambda b, h, i: (b, 0, h)),
                pl.BlockSpec((1, n_tiles, dv, tk), lambda b, h, i: (b, 0, h, 0))]
    in_specs += [_const_spec(a.shape) for a in extra]
    s_shape = (2, tk, tq) if n_softmax == 1 else (2, n_softmax, tk, tq)
    scratch = [pltpu.VMEM(s_shape, F32)]
    for _ in range(n_softmax):
        scratch += [pltpu.VMEM((1, tq), F32), pltpu.VMEM((1, tq), F32), pltpu.VMEM((dv, tq), F32)]
    return pl.pallas_call(
        kernel_fn,
        grid=(bsz, heads, s // tq),
        in_specs=in_specs,
        out_specs=pl.BlockSpec((1, dv, tq), lambda b, h, i: (b, h, i)),
        out_shape=jax.ShapeDtypeStruct((bsz, heads * dv, s), BF16),
        scratch_shapes=scratch,
        compiler_params=_params(("parallel", "parallel", "arbitrary")),
        name=name,
    )(qt, k, vt, *extra)


def _out_kernel(oa_ref, od_ref, ga_ref, gd_ref, m_ref, x_ref, gate_ref, woa_ref, wob_ref, wout_ref,
                g_ref, b_ref, o_ref):
    ua = (oa_ref[0].astype(F32) * ga_ref[0].astype(F32)).astype(BF16)
    ud = (od_ref[0].astype(F32) * gd_ref[0].astype(F32)).astype(BF16)
    ya = _dot(woa_ref[...], ua)
    yd = _dot(wob_ref[...], ud)
    mix = m_ref[0, :D_MODEL, :].astype(F32) * ya + m_ref[0, D_MODEL:, :].astype(F32) * yd
    y = _tn_dot(mix.astype(BF16), wout_ref[...])
    z = ALPHA * x_ref[0] + gate_ref[0] * y
    mu = jnp.mean(z, axis=-1, keepdims=True)
    zc = z - mu
    var = jnp.mean(zc * zc, axis=-1, keepdims=True)
    o_ref[0] = zc * lax.rsqrt(var + LN_EPS) * g_ref[...] + b_ref[...]


def _output(oa, od, ga, gd, mt, x, gate, w):
    bsz, s, d = x.shape
    tf = min(OUT_TOKENS, s)
    assert s % tf == 0
    feat = lambda rows: pl.BlockSpec((1, rows, tf), lambda b, i: (b, 0, i))
    return pl.pallas_call(
        _out_kernel,
        grid=(bsz, s // tf),
        in_specs=[feat(MLA_WIDTH), feat(DIFF_WIDTH), feat(MLA_WIDTH), feat(DIFF_WIDTH), feat(2 * d),
                  pl.BlockSpec((1, tf, d), lambda b, i: (b, i, 0)),
                  pl.BlockSpec((1, 1, d), lambda b, i: (b, 0, 0)),
                  _const_spec(w["woa"].shape), _const_spec(w["wob"].shape), _const_spec(w["wout"].shape),
                  _const_spec(w["ln_g"].shape), _const_spec(w["ln_b"].shape)],
        out_specs=pl.BlockSpec((1, tf, d), lambda b, i: (b, i, 0)),
        out_shape=jax.ShapeDtypeStruct((bsz, s, d), F32),
        compiler_params=_params(("parallel", "parallel")),
        name="output",
    )(oa, od, ga, gd, mt, x, gate, w["woa"], w["wob"], w["wout"], w["ln_g"], w["ln_b"])


def _prep_weights(w_in, mla_q_norm, mla_kv_norm, w_uq, w_ukv, w_oa, w_ob, w_out, ln_g, ln_b):
    o = IN_OFFSETS
    cols = lambda i: w_in[:, o[i]:o[i + 1]]
    w_cq, w_ckv, w_kr, w_ga, w_dq, w_dk, w_dv, w_gd, w_mg = (cols(i) for i in range(9))
    lanes = V7X_LANES
    pad_kr = jnp.zeros((D_MODEL, lanes - MLA_ROPE), F32)
    dk_perm = np.concatenate([g * DIFF_HD + _swap_perm(DIFF_HD) for g in range(2 * DIFF_HEADS)])
    wnat = jnp.concatenate([w_cq, w_ckv, w_dk, w_dk[:, dk_perm], w_kr, pad_kr,
                            w_kr[:, _swap_perm(MLA_ROPE)], pad_kr], axis=1)
    assert wnat.shape[1] == _NAT_COLS
    wt = jnp.concatenate([w_dq, w_dv, w_ga, w_gd, w_mg], axis=1).T
    assert wt.shape[0] == _T_ROWS

    uq = w_uq.reshape(MLA_Q_LORA, MLA_HEADS, MLA_NOPE + MLA_ROPE)
    uq = jnp.pad(uq, ((0, 0), (0, 0), (0, lanes - MLA_NOPE - MLA_ROPE)))
    wuq = uq.reshape(MLA_Q_LORA, MLA_HEADS * lanes).T
    ukv = w_ukv.reshape(MLA_KV_LORA, MLA_HEADS, MLA_NOPE + MLA_V)
    wuk = jnp.pad(ukv[:, :, :MLA_NOPE], ((0, 0), (0, 0), (0, lanes - MLA_NOPE)))
    wuk = wuk.reshape(MLA_KV_LORA, MLA_HEADS * lanes)
    wuv = ukv[:, :, MLA_NOPE:].reshape(MLA_KV_LORA, MLA_WIDTH).T
    ekr = np.zeros((lanes, MLA_HEADS * lanes), np.float32)
    for hh in range(MLA_HEADS):
        ekr[np.arange(MLA_ROPE), hh * lanes + MLA_NOPE + np.arange(MLA_ROPE)] = 1.0
    return {
        "wnat": wnat.astype(BF16), "wt": wt.astype(BF16), "wt_ctx": w_dv.T.astype(BF16),
        "qn": mla_q_norm[None, :], "kvn": mla_kv_norm[None, :],
        "wuq": wuq.astype(BF16), "wuk": wuk.astype(BF16), "wuv": wuv.astype(BF16),
        "ekr": jnp.asarray(ekr, BF16),
        "woa": w_oa.T.astype(BF16), "wob": w_ob.T.astype(BF16), "wout": w_out.astype(BF16),
        "ln_g": ln_g[None, :], "ln_b": ln_b[None, :],
    }


def kernel(x, c, ctx, c_ctx, w_mod, b_mod, w_in, mla_q_norm, mla_kv_norm, w_uq, w_ukv, diff_lambda,
           diff_subln, w_oa, w_ob, w_out, ln_g, ln_b):
    bsz, s, d = x.shape
    n_ctx = ctx.shape[1]
    assert w_mod.shape[0] == DEPTH == 1 and d == D_MODEL and s % GRID_W == 0 and bsz + 1 <= MOD_ROWS

    cc = jnp.concatenate([c, c_ctx[None, :], jnp.zeros((MOD_ROWS - bsz - 1, d), F32)], axis=0)
    mod = _modulation(cc, w_mod[0], b_mod[0][None, :])
    shift, scale, gate = (mod[:bsz, i * d:(i + 1) * d][:, None, :] for i in range(3))
    shift_c, scale_c = (mod[bsz:bsz + 1, i * d:(i + 1) * d][:, None, :] for i in range(2))

    w = _prep_weights(w_in[0], mla_q_norm[0], mla_kv_norm[0], w_uq[0], w_ukv[0], w_oa[0], w_ob[0], w_out[0],
                      ln_g[0], ln_b[0])
    tabn, tabt = _tables(s, rotary=True)
    tabn_c, _ = _tables(n_ctx, rotary=False)

    n_total = s + n_ctx
    ctx_tokens = min(PROJ_TOKENS, n_ctx)
    assert s % ctx_tokens == 0
    shared = _projection(ctx, scale_c, shift_c, w, tabn_c, None, n_total=n_total,
                         tile_offset=s // ctx_tokens)
    ka, vta, kd, vtd, qta, qtd, gat, gdt, mt = _projection(x, scale, shift, w, tabn, tabt, n_total=n_total,
                                                           tile_offset=0, shared=shared)

    oa = _attention(qta, ka, vta, heads=MLA_HEADS, dv=MLA_V, kernel_fn=_mla_attn_kernel, n_softmax=1,
                    name="attention_mla")
    lam_a = diff_lambda[0][0::2]
    lam_b = diff_lambda[0][1::2]
    od = _attention(qtd, kd, vtd, heads=DIFF_HEADS, dv=2 * DIFF_HD,
                    extra=(lam_a, lam_b, diff_subln[0][:, None]), kernel_fn=_diff_attn_kernel, n_softmax=2,
                    name="attention_diff")
    return _output(oa, od, gat, gdt, mt, x, gate, w)
```

```python
import functools
import math

import numpy as np
import jax
import jax.numpy as jnp
from jax import lax
from jax.experimental import pallas as pl
from jax.experimental.pallas import tpu as pltpu

F32 = jnp.float32
BF16 = jnp.bfloat16

D_MODEL = 1024
GRID_W = 64
ROPE_THETA = 10000.0
LN_EPS = 1e-5
RMS_EPS = 1e-6
SUBLN_EPS = 1e-5
MLA_HEADS = 8
MLA_NOPE = 64
MLA_ROPE = 32
MLA_V = 64
MLA_Q_LORA = 384
MLA_KV_LORA = 256
MLA_WIDTH = MLA_HEADS * MLA_V
MLA_SCALE = (MLA_NOPE + MLA_ROPE) ** -0.5
DIFF_HEADS = 4
DIFF_HD = 64
DIFF_WIDTH = DIFF_HEADS * 2 * DIFF_HD
DIFF_SCALE = DIFF_HD ** -0.5
DEPTH = 1
ALPHA = (2 * DEPTH) ** 0.25
LAM_INIT = 0.8 - 0.6 * math.exp(-0.3 * 0)
IN_SPLITS = (MLA_Q_LORA, MLA_KV_LORA, MLA_ROPE, MLA_WIDTH, DIFF_WIDTH, DIFF_WIDTH, DIFF_WIDTH, DIFF_WIDTH,
             2 * D_MODEL)
IN_OFFSETS = tuple(int(o) for o in np.cumsum((0,) + IN_SPLITS))
LOG2E = math.log2(math.e)

V7X_LANES = 128
V7X_VMEM_LIMIT_BYTES = 56 * 2**20
MOD_ROWS = 16
DENOM_ROWS = 16

PROJ_TOKENS = 256
KV_TILE = 256
Q_TILE = 512
OUT_TOKENS = 512

_NAT_CQ = (0, 384)
_NAT_CKV = (384, 640)
_NAT_DK = (640, 1152)
_NAT_DKSW = (1152, 1664)
_NAT_KR = (1664, 1792)
_NAT_KRSW = (1792, 1920)
_NAT_COLS = 1920
_T_DQ = (0, 512)
_T_DV = (512, 1024)
_T_GA = (1024, 1536)
_T_GD = (1536, 2048)
_T_MERGE = (2048, 4096)
_T_ROWS = 4096


def _nt_dot(a, b):
    return lax.dot_general(a, b, (((1,), (1,)), ((), ())), preferred_element_type=F32)


def _tn_dot(a, b):
    return lax.dot_general(a, b, (((0,), (0,)), ((), ())), preferred_element_type=F32)


def _dot(a, b):
    return jnp.dot(a, b, preferred_element_type=F32)


def _const_spec(shape):
    nd = len(shape)
    return pl.BlockSpec(shape, lambda *_: (0,) * nd)


def _params(semantics):
    return pltpu.CompilerParams(dimension_semantics=semantics, vmem_limit_bytes=V7X_VMEM_LIMIT_BYTES)


def _rope_tables(seq, width):
    h2 = width // 4
    pos = jnp.arange(seq, dtype=jnp.int32)
    row = (pos // GRID_W).astype(F32)
    col = (pos % GRID_W).astype(F32)
    inv = ROPE_THETA ** (-jnp.arange(h2, dtype=F32) / h2)

    def part(p):
        ang = p[:, None] * inv[None, :]
        c, s = jnp.cos(ang), jnp.sin(ang)
        return jnp.concatenate([c, c], -1), jnp.concatenate([-s, s], -1)

    cr, sr = part(row)
    cc, sc = part(col)
    return jnp.concatenate([cr, cc], -1), jnp.concatenate([sr, sc], -1)


def _swap_perm(width):
    half, h2 = width // 2, width // 4
    j = np.arange(width)
    return (j // half) * half + ((j % half) + h2) % half


def _swap_rows(x, width):
    h2 = width // 4
    p = [x[i * h2:(i + 1) * h2] for i in range(4)]
    return jnp.concatenate([p[1], p[0], p[3], p[2]], axis=0)


def _tables(seq, rotary):
    if rotary:
        c64, s64 = _rope_tables(seq, DIFF_HD)
        c32, s32 = _rope_tables(seq, MLA_ROPE)
    else:
        c64, s64 = jnp.ones((seq, DIFF_HD), F32), jnp.zeros((seq, DIFF_HD), F32)
        c32, s32 = jnp.ones((seq, MLA_ROPE), F32), jnp.zeros((seq, MLA_ROPE), F32)
    pad = jnp.zeros((seq, V7X_LANES - MLA_ROPE), F32)
    nat = jnp.concatenate([c64, c64, s64, s64, c32, pad, s32, pad], axis=-1)
    tr = jnp.concatenate([c32, s32, c64, s64], axis=-1).T
    return nat, tr


def _mod_kernel(c_ref, w_ref, b_ref, o_ref):
    c = c_ref[...]
    s = c * jax.nn.sigmoid(c)
    o_ref[...] = jnp.dot(s, w_ref[...], precision=lax.Precision.HIGHEST,
                         preferred_element_type=F32) + b_ref[...]


def _modulation(cc, w_mod, b_mod):
    rows, d = cc.shape
    n = w_mod.shape[1]
    return pl.pallas_call(
        _mod_kernel,
        grid=(n // d,),
        in_specs=[pl.BlockSpec((rows, d), lambda j: (0, 0)),
                  pl.BlockSpec((d, d), lambda j: (0, j)),
                  pl.BlockSpec((1, d), lambda j: (0, j))],
        out_specs=pl.BlockSpec((rows, d), lambda j: (0, j)),
        out_shape=jax.ShapeDtypeStruct((rows, n), F32),
        compiler_params=_params(("arbitrary",)),
        name="modulation",
    )(cc, w_mod, b_mod)


def _rms_rows(c, w):
    return c * lax.rsqrt(jnp.mean(c * c, axis=-1, keepdims=True) + RMS_EPS) * w


def _proj_kernel(*refs, latent, n_kv):
    if latent:
        (x_ref, scale_ref, shift_ref, wnat_ref, wt_ref, qn_ref, kvn_ref, wuq_ref, wuk_ref, ekr_ref, wuv_ref,
         tabn_ref, tabt_ref, _, _, _, _,
         ka_ref, vta_ref, kd_ref, vtd_ref, qta_ref, qtd_ref, gat_ref, gdt_ref, mt_ref) = refs
    else:
        (x_ref, scale_ref, shift_ref, wnat_ref, wt_ref, kvn_ref, wuk_ref, ekr_ref, wuv_ref, tabn_ref,
         ka_ref, vta_ref, kd_ref, vtd_ref) = refs
    tk = vta_ref.shape[-1]
    lanes = V7X_LANES

    h = (x_ref[0] * (1.0 + scale_ref[0]) + shift_ref[0]).astype(BF16)
    pn = _dot(h, wnat_ref[...])

    ckv = _rms_rows(pn[:, _NAT_CKV[0]:_NAT_CKV[1]], kvn_ref[...]).astype(BF16)
    kr = (pn[:, _NAT_KR[0]:_NAT_KR[1]] * tabn_ref[:, 2 * lanes:3 * lanes]
          + pn[:, _NAT_KRSW[0]:_NAT_KRSW[1]] * tabn_ref[:, 3 * lanes:4 * lanes])
    ka = _dot(ckv, wuk_ref[...]) + _dot(kr.astype(BF16), ekr_ref[...])
    ka_ref[0] = ka.astype(BF16)
    vta = _nt_dot(wuv_ref[...], ckv).astype(BF16)
    for j in range(n_kv):
        vta_ref[0, j] = vta[:, j * tk:(j + 1) * tk]

    for hh in range(DIFF_HEADS):
        a = _NAT_DK[0] + hh * lanes
        b = _NAT_DKSW[0] + hh * lanes
        kd = pn[:, a:a + lanes] * tabn_ref[:, 0:lanes] + pn[:, b:b + lanes] * tabn_ref[:, lanes:2 * lanes]
        kd_ref[0, :, hh * lanes:(hh + 1) * lanes] = kd.astype(BF16)

    dv_rows = _T_DV if latent else (0, DIFF_WIDTH)
    vtd = _nt_dot(wt_ref[dv_rows[0]:dv_rows[1], :], h).astype(BF16)
    for j in range(n_kv):
        vtd_ref[0, j] = vtd[:, j * tk:(j + 1) * tk]

    if not latent:
        return

    cq = _rms_rows(pn[:, _NAT_CQ[0]:_NAT_CQ[1]], qn_ref[...]).astype(BF16)
    qta = _nt_dot(wuq_ref[...], cq) * (MLA_SCALE * LOG2E)
    c32, s32 = tabt_ref[0:MLA_ROPE, :], tabt_ref[MLA_ROPE:2 * MLA_ROPE, :]
    for hh in range(MLA_HEADS):
        base = hh * lanes
        r0, r1 = base + MLA_NOPE, base + MLA_NOPE + MLA_ROPE
        qta_ref[0, base:r0, :] = qta[base:r0].astype(BF16)
        r = qta[r0:r1]
        qta_ref[0, r0:r1, :] = (r * c32 + _swap_rows(r, MLA_ROPE) * s32).astype(BF16)
        qta_ref[0, r1:base + lanes, :] = qta[r1:base + lanes].astype(BF16)

    qtd = _nt_dot(wt_ref[_T_DQ[0]:_T_DQ[1], :], h) * (DIFF_SCALE * LOG2E)
    c64 = tabt_ref[2 * MLA_ROPE:2 * MLA_ROPE + DIFF_HD, :]
    s64 = tabt_ref[2 * MLA_ROPE + DIFF_HD:2 * MLA_ROPE + 2 * DIFF_HD, :]
    for g in range(2 * DIFF_HEADS):
        blk = qtd[g * DIFF_HD:(g + 1) * DIFF_HD]
        qtd_ref[0, g * DIFF_HD:(g + 1) * DIFF_HD, :] = (blk * c64 + _swap_rows(blk, DIFF_HD) * s64).astype(BF16)

    ga = _nt_dot(wt_ref[_T_GA[0]:_T_GA[1], :], h)
    gat_ref[0] = (ga * jax.nn.sigmoid(ga)).astype(BF16)
    gd = _nt_dot(wt_ref[_T_GD[0]:_T_GD[1], :], h)
    gdt_ref[0] = (gd * jax.nn.sigmoid(gd)).astype(BF16)
    mt_ref[0] = jax.nn.sigmoid(_nt_dot(wt_ref[_T_MERGE[0]:_T_MERGE[1], :], h)).astype(BF16)


def _projection(tokens, scale, shift, w, tabn, tabt, *, n_total, tile_offset, shared=None):
    latent = shared is not None
    bsz, t, d = tokens.shape
    ts = min(PROJ_TOKENS, t)
    tk = KV_TILE
    n_kv = ts // tk
    assert t % ts == 0 and ts % tk == 0 and n_total % tk == 0
    off = tile_offset
    per_batch = scale.shape[0] == bsz
    mod_map = (lambda b, i: (b, 0, 0)) if per_batch else (lambda b, i: (0, 0, 0))
    wt = w["wt"] if latent else w["wt_ctx"]

    in_arrays = [tokens, scale, shift, w["wnat"], wt]
    in_specs = [pl.BlockSpec((1, ts, d), lambda b, i: (b, i, 0)),
                pl.BlockSpec((1, 1, d), mod_map), pl.BlockSpec((1, 1, d), mod_map),
                _const_spec(w["wnat"].shape), _const_spec(wt.shape)]
    names = ("qn", "kvn", "wuq", "wuk", "ekr", "wuv") if latent else ("kvn", "wuk", "ekr", "wuv")
    for name in names:
        in_arrays.append(w[name])
        in_specs.append(_const_spec(w[name].shape))
    in_arrays.append(tabn)
    in_specs.append(pl.BlockSpec((ts, tabn.shape[1]), lambda b, i: (i, 0)))
    aliases = {}
    if latent:
        in_arrays.append(tabt)
        in_specs.append(pl.BlockSpec((tabt.shape[0], ts), lambda b, i: (0, i)))
        for j, buf in enumerate(shared):
            aliases[len(in_arrays)] = j
            in_arrays.append(buf)
            in_specs.append(pl.BlockSpec(memory_space=pl.ANY))

    hp = MLA_HEADS * V7X_LANES
    out_shape = [jax.ShapeDtypeStruct((bsz, n_total, hp), BF16),
                 jax.ShapeDtypeStruct((bsz, n_total // tk, MLA_WIDTH, tk), BF16),
                 jax.ShapeDtypeStruct((bsz, n_total, DIFF_WIDTH), BF16),
                 jax.ShapeDtypeStruct((bsz, n_total // tk, DIFF_WIDTH, tk), BF16)]
    out_specs = [pl.BlockSpec((1, ts, hp), lambda b, i: (b, i + off, 0)),
                 pl.BlockSpec((1, n_kv, MLA_WIDTH, tk), lambda b, i: (b, i + off, 0, 0)),
                 pl.BlockSpec((1, ts, DIFF_WIDTH), lambda b, i: (b, i + off, 0)),
                 pl.BlockSpec((1, n_kv, DIFF_WIDTH, tk), lambda b, i: (b, i + off, 0, 0))]
    if latent:
        for rows in (hp, DIFF_WIDTH, MLA_WIDTH, DIFF_WIDTH, 2 * D_MODEL):
            out_shape.append(jax.ShapeDtypeStruct((bsz, rows, t), BF16))
            out_specs.append(pl.BlockSpec((1, rows, ts), lambda b, i: (b, 0, i)))

    return pl.pallas_call(
        functools.partial(_proj_kernel, latent=latent, n_kv=n_kv),
        grid=(bsz, t // ts),
        in_specs=in_specs,
        out_specs=out_specs,
        out_shape=out_shape,
        input_output_aliases=aliases,
        compiler_params=_params(("parallel", "parallel")),
        name="projection_latent" if latent else "projection_context",
    )(*in_arrays)


def _ones_rows(tk):
    rows = lax.broadcasted_iota(jnp.int32, (DENOM_ROWS, tk), 0)
    return jnp.where(rows == 0, 1.0, 0.0).astype(BF16)


def _softmax_step(s, vts, ones, m_sc, acc_sc):
    tk = ones.shape[1]
    m_old = m_sc[...]
    m_new = jnp.maximum(m_old, jnp.max(s, axis=0, keepdims=True))
    alpha = jnp.exp2(m_old - m_new)
    p = jnp.exp2(s - m_new).astype(BF16)
    pv = None
    for i, vt in enumerate(vts):
        d = _dot(jnp.concatenate([vt, ones], axis=0), p[i * tk:(i + 1) * tk])
        pv = d if pv is None else pv + d
    acc_sc[...] = alpha * acc_sc[...] + pv
    m_sc[...] = m_new


def _init_stats(m_sc, acc_sc):
    m_sc[...] = jnp.full(m_sc.shape, -jnp.inf, F32)
    acc_sc[...] = jnp.zeros(acc_sc.shape, F32)


def _normalised(acc_sc, dv):
    acc = acc_sc[...]
    return acc[:dv] / acc[dv:dv + 1]


def _kv_group(n_tiles):
    return next(g for g in (3, 2, 1) if n_tiles % g == 0)


def _kv_pipeline(n_tiles, scores, process):
    scores(0, 0)
    n_pairs = (n_tiles - 1) // 2

    def body(t, carry):
        scores(2 * t + 1, 1)
        process(2 * t, 0)
        scores(2 * t + 2, 0)
        process(2 * t + 1, 1)
        return carry

    lax.fori_loop(0, n_pairs, body, 0)
    last = 2 * n_pairs
    if n_tiles % 2 == 0:
        scores(last + 1, 1)
        process(last, 0)
        process(last + 1, 1)
    else:
        process(last, 0)


def _kv_rows(k_ref, j, rows):
    off = j * rows if isinstance(j, int) else pl.multiple_of(j * rows, rows)
    return k_ref[0, pl.ds(off, rows), :]


def _mla_attn_kernel(q_ref, k_ref, vt_ref, o_ref, s_sc, m_sc, acc_sc):
    n_tiles, dv, tk = vt_ref.shape[1:]
    g = _kv_group(n_tiles)
    q = q_ref[0]
    ones = _ones_rows(tk)
    _init_stats(m_sc, acc_sc)

    def scores(j, slot):
        s_sc[slot] = _dot(_kv_rows(k_ref, j, g * tk), q)

    def process(j, slot):
        _softmax_step(s_sc[slot], [vt_ref[0, j * g + i] for i in range(g)], ones, m_sc, acc_sc)

    _kv_pipeline(n_tiles // g, scores, process)
    o_ref[0] = _normalised(acc_sc, dv).astype(o_ref.dtype)


def _diff_attn_kernel(q_ref, k_ref, vt_ref, lam_a_ref, lam_b_ref, subln_ref, o_ref, s_sc, m1, a1, m2, a2):
    n_tiles, dv, tk = vt_ref.shape[1:]
    g = _kv_group(n_tiles)
    q = q_ref[0]
    zeros = jnp.zeros((DIFF_HD, q.shape[1]), q.dtype)
    q1 = jnp.concatenate([q[:DIFF_HD], zeros], axis=0)
    q2 = jnp.concatenate([zeros, q[DIFF_HD:]], axis=0)
    ones = _ones_rows(tk)
    _init_stats(m1, a1)
    _init_stats(m2, a2)

    def scores(j, slot):
        k = _kv_rows(k_ref, j, g * tk)
        s_sc[slot, 0] = _dot(k, q1)
        s_sc[slot, 1] = _dot(k, q2)

    def process(j, slot):
        vts = [vt_ref[0, j * g + i] for i in range(g)]
        _softmax_step(s_sc[slot, 0], vts, ones, m1, a1)
        _softmax_step(s_sc[slot, 1], vts, ones, m2, a2)

    _kv_pipeline(n_tiles // g, scores, process)

    e = jnp.exp(jnp.sum(lam_a_ref[...] * lam_b_ref[...], axis=1, keepdims=True))
    sign = jnp.where(lax.broadcasted_iota(jnp.int32, e.shape, 0) == 0, 1.0, -1.0)
    lam = jnp.sum(e * sign, axis=0, keepdims=True) + LAM_INIT
    o = _normalised(a1, dv) - lam * _normalised(a2, dv)
    o = o * lax.rsqrt(jnp.mean(o * o, axis=0, keepdims=True) + SUBLN_EPS)
    o_ref[0] = ((o * subln_ref[...]) * (1.0 - LAM_INIT)).astype(o_ref.dtype)


def _attention(qt, k, vt, *, heads, dv, extra=(), kernel_fn, n_softmax, name):
    bsz, _, s = qt.shape
    t = k.shape[1]
    tq = min(Q_TILE, s)
    n_tiles, tk = vt.shape[1], vt.shape[-1]
    lanes = V7X_LANES
    assert s % tq == 0 and n_tiles * tk == t
    in_specs = [pl.BlockSpec((1, lanes, tq), lambda b, h, i: (b, h, i)),
                pl.BlockSpec((1, t, lanes), lambda b, h, i: (b, 0, h)),
                pl.BlockSpec((1, n_tiles, dv, tk), lambda b, h, i: (b, 0, h, 0))]
    in_specs += [_const_spec(a.shape) for a in extra]
    rows = _kv_group(n_tiles) * tk
    s_shape = (2, rows, tq) if n_softmax == 1 else (2, n_softmax, rows, tq)
    scratch = [pltpu.VMEM(s_shape, F32)]
    for _ in range(n_softmax):
        scratch += [pltpu.VMEM((1, tq), F32), pltpu.VMEM((dv + DENOM_ROWS, tq), F32)]
    return pl.pallas_call(
        kernel_fn,
        grid=(bsz, heads, s // tq),
        in_specs=in_specs,
        out_specs=pl.BlockSpec((1, dv, tq), lambda b, h, i: (b, h, i)),
        out_shape=jax.ShapeDtypeStruct((bsz, heads * dv, s), BF16),
        scratch_shapes=scratch,
        compiler_params=_params(("parallel", "parallel", "arbitrary")),
        name=name,
    )(qt, k, vt, *extra)


def _out_kernel(oa_ref, od_ref, ga_ref, gd_ref, m_ref, x_ref, gate_ref, woa_ref, wob_ref, wout_ref,
                g_ref, b_ref, o_ref):
    ua = (oa_ref[0].astype(F32) * ga_ref[0].astype(F32)).astype(BF16)
    ud = (od_ref[0].astype(F32) * gd_ref[0].astype(F32)).astype(BF16)
    ya = _dot(woa_ref[...], ua)
    yd = _dot(wob_ref[...], ud)
    mix = m_ref[0, :D_MODEL, :].astype(F32) * ya + m_ref[0, D_MODEL:, :].astype(F32) * yd
    y = _tn_dot(mix.astype(BF16), wout_ref[...])
    z = ALPHA * x_ref[0] + gate_ref[0] * y
    mu = jnp.mean(z, axis=-1, keepdims=True)
    zc = z - mu
    var = jnp.mean(zc * zc, axis=-1, keepdims=True)
    o_ref[0] = zc * lax.rsqrt(var + LN_EPS) * g_ref[...] + b_ref[...]


def _output(oa, od, ga, gd, mt, x, gate, w):
    bsz, s, d = x.shape
    tf = min(OUT_TOKENS, s)
    assert s % tf == 0
    feat = lambda rows: pl.BlockSpec((1, rows, tf), lambda b, i: (b, 0, i))
    return pl.pallas_call(
        _out_kernel,
        grid=(bsz, s // tf),
        in_specs=[feat(MLA_WIDTH), feat(DIFF_WIDTH), feat(MLA_WIDTH), feat(DIFF_WIDTH), feat(2 * d),
                  pl.BlockSpec((1, tf, d), lambda b, i: (b, i, 0)),
                  pl.BlockSpec((1, 1, d), lambda b, i: (b, 0, 0)),
                  _const_spec(w["woa"].shape), _const_spec(w["wob"].shape), _const_spec(w["wout"].shape),
                  _const_spec(w["ln_g"].shape), _const_spec(w["ln_b"].shape)],
        out_specs=pl.BlockSpec((1, tf, d), lambda b, i: (b, i, 0)),
        out_shape=jax.ShapeDtypeStruct((bsz, s, d), F32),
        compiler_params=_params(("parallel", "parallel")),
        name="output",
    )(oa, od, ga, gd, mt, x, gate, w["woa"], w["wob"], w["wout"], w["ln_g"], w["ln_b"])


def _prep_weights(w_in, mla_q_norm, mla_kv_norm, w_uq, w_ukv, w_oa, w_ob, w_out, ln_g, ln_b):
    o = IN_OFFSETS
    cols = lambda i: w_in[:, o[i]:o[i + 1]]
    w_cq, w_ckv, w_kr, w_ga, w_dq, w_dk, w_dv, w_gd, w_mg = (cols(i) for i in range(9))
    lanes = V7X_LANES
    pad_kr = jnp.zeros((D_MODEL, lanes - MLA_ROPE), F32)
    dk_perm = np.concatenate([g * DIFF_HD + _swap_perm(DIFF_HD) for g in range(2 * DIFF_HEADS)])
    wnat = jnp.concatenate([w_cq, w_ckv, w_dk, w_dk[:, dk_perm], w_kr, pad_kr,
                            w_kr[:, _swap_perm(MLA_ROPE)], pad_kr], axis=1)
    assert wnat.shape[1] == _NAT_COLS
    wt = jnp.concatenate([w_dq, w_dv, w_ga, w_gd, w_mg], axis=1).T
    assert wt.shape[0] == _T_ROWS

    uq = w_uq.reshape(MLA_Q_LORA, MLA_HEADS, MLA_NOPE + MLA_ROPE)
    uq = jnp.pad(uq, ((0, 0), (0, 0), (0, lanes - MLA_NOPE - MLA_ROPE)))
    wuq = uq.reshape(MLA_Q_LORA, MLA_HEADS * lanes).T
    ukv = w_ukv.reshape(MLA_KV_LORA, MLA_HEADS, MLA_NOPE + MLA_V)
    wuk = jnp.pad(ukv[:, :, :MLA_NOPE], ((0, 0), (0, 0), (0, lanes - MLA_NOPE)))
    wuk = wuk.reshape(MLA_KV_LORA, MLA_HEADS * lanes)
    wuv = ukv[:, :, MLA_NOPE:].reshape(MLA_KV_LORA, MLA_WIDTH).T
    ekr = np.zeros((lanes, MLA_HEADS * lanes), np.float32)
    for hh in range(MLA_HEADS):
        ekr[np.arange(MLA_ROPE), hh * lanes + MLA_NOPE + np.arange(MLA_ROPE)] = 1.0
    return {
        "wnat": wnat.astype(BF16), "wt": wt.astype(BF16), "wt_ctx": w_dv.T.astype(BF16),
        "qn": mla_q_norm[None, :], "kvn": mla_kv_norm[None, :],
        "wuq": wuq.astype(BF16), "wuk": wuk.astype(BF16), "wuv": wuv.astype(BF16),
        "ekr": jnp.asarray(ekr, BF16),
        "woa": w_oa.T.astype(BF16), "wob": w_ob.T.astype(BF16), "wout": w_out.astype(BF16),
        "ln_g": ln_g[None, :], "ln_b": ln_b[None, :],
    }


def kernel(x, c, ctx, c_ctx, w_mod, b_mod, w_in, mla_q_norm, mla_kv_norm, w_uq, w_ukv, diff_lambda,
           diff_subln, w_oa, w_ob, w_out, ln_g, ln_b):
    bsz, s, d = x.shape
    n_ctx = ctx.shape[1]
    assert w_mod.shape[0] == DEPTH == 1 and d == D_MODEL and s % GRID_W == 0 and bsz + 1 <= MOD_ROWS

    cc = jnp.concatenate([c, c_ctx[None, :], jnp.zeros((MOD_ROWS - bsz - 1, d), F32)], axis=0)
    mod = _modulation(cc, w_mod[0], b_mod[0][None, :])
    shift, scale, gate = (mod[:bsz, i * d:(i + 1) * d][:, None, :] for i in range(3))
    shift_c, scale_c = (mod[bsz:bsz + 1, i * d:(i + 1) * d][:, None, :] for i in range(2))

    w = _prep_weights(w_in[0], mla_q_norm[0], mla_kv_norm[0], w_uq[0], w_ukv[0], w_oa[0], w_ob[0], w_out[0],
                      ln_g[0], ln_b[0])
    tabn, tabt = _tables(s, rotary=True)
    tabn_c, _ = _tables(n_ctx, rotary=False)

    n_total = s + n_ctx
    ctx_tokens = min(PROJ_TOKENS, n_ctx)
    assert s % ctx_tokens == 0
    shared = _projection(ctx, scale_c, shift_c, w, tabn_c, None, n_total=n_total,
                         tile_offset=s // ctx_tokens)
    ka, vta, kd, vtd, qta, qtd, gat, gdt, mt = _projection(x, scale, shift, w, tabn, tabt, n_total=n_total,
                                                           tile_offset=0, shared=shared)

    oa = _attention(qta, ka, vta, heads=MLA_HEADS, dv=MLA_V, kernel_fn=_mla_attn_kernel, n_softmax=1,
                    name="attention_mla")
    lam_a = diff_lambda[0][0::2]
    lam_b = diff_lambda[0][1::2]
    od = _attention(qtd, kd, vtd, heads=DIFF_HEADS, dv=2 * DIFF_HD,
                    extra=(lam_a, lam_b, diff_subln[0][:, None]), kernel_fn=_diff_attn_kernel, n_softmax=2,
                    name="attention_diff")
    return _output(oa, od, gat, gdt, mt, x, gate, w)
```

```python
import functools
import math

import numpy as np
import jax
import jax.numpy as jnp
from jax import lax
from jax.experimental import pallas as pl
from jax.experimental.pallas import tpu as pltpu

F32 = jnp.float32
BF16 = jnp.bfloat16

D_MODEL = 1024
GRID_W = 64
ROPE_THETA = 10000.0
LN_EPS = 1e-5
RMS_EPS = 1e-6
SUBLN_EPS = 1e-5
MLA_HEADS = 8
MLA_NOPE = 64
MLA_ROPE = 32
MLA_V = 64
MLA_Q_LORA = 384
MLA_KV_LORA = 256
MLA_WIDTH = MLA_HEADS * MLA_V
MLA_SCALE = (MLA_NOPE + MLA_ROPE) ** -0.5
DIFF_HEADS = 4
DIFF_HD = 64
DIFF_WIDTH = DIFF_HEADS * 2 * DIFF_HD
DIFF_SCALE = DIFF_HD ** -0.5
DEPTH = 1
ALPHA = (2 * DEPTH) ** 0.25
LAM_INIT = 0.8 - 0.6 * math.exp(-0.3 * 0)
IN_SPLITS = (MLA_Q_LORA, MLA_KV_LORA, MLA_ROPE, MLA_WIDTH, DIFF_WIDTH, DIFF_WIDTH, DIFF_WIDTH, DIFF_WIDTH,
             2 * D_MODEL)
IN_OFFSETS = tuple(int(o) for o in np.cumsum((0,) + IN_SPLITS))
LOG2E = math.log2(math.e)

V7X_LANES = 128
V7X_VMEM_LIMIT_BYTES = 56 * 2**20
MOD_ROWS = 16
DENOM_ROWS = 16

PROJ_TOKENS = 256
KV_TILE = 256
Q_TILE = 512

_NAT_CQ = (0, 384)
_NAT_CKV = (384, 640)
_NAT_DK = (640, 1152)
_NAT_DKSW = (1152, 1664)
_NAT_KR = (1664, 1792)
_NAT_KRSW = (1792, 1920)
_NAT_COLS = 1920
_T_DQ = (0, 512)
_T_DV = (512, 1024)
_T_GA = (1024, 1536)
_T_GD = (1536, 2048)
_T_MERGE = (2048, 4096)
_T_ROWS = 4096


def _nt_dot(a, b):
    return lax.dot_general(a, b, (((1,), (1,)), ((), ())), preferred_element_type=F32)


def _tn_dot(a, b):
    return lax.dot_general(a, b, (((0,), (0,)), ((), ())), preferred_element_type=F32)


def _dot(a, b):
    return jnp.dot(a, b, preferred_element_type=F32)


def _const_spec(shape):
    nd = len(shape)
    return pl.BlockSpec(shape, lambda *_: (0,) * nd)


def _params(semantics):
    return pltpu.CompilerParams(dimension_semantics=semantics, vmem_limit_bytes=V7X_VMEM_LIMIT_BYTES)


def _rope_tables(seq, width):
    h2 = width // 4
    pos = jnp.arange(seq, dtype=jnp.int32)
    row = (pos // GRID_W).astype(F32)
    col = (pos % GRID_W).astype(F32)
    inv = ROPE_THETA ** (-jnp.arange(h2, dtype=F32) / h2)

    def part(p):
        ang = p[:, None] * inv[None, :]
        c, s = jnp.cos(ang), jnp.sin(ang)
        return jnp.concatenate([c, c], -1), jnp.concatenate([-s, s], -1)

    cr, sr = part(row)
    cc, sc = part(col)
    return jnp.concatenate([cr, cc], -1), jnp.concatenate([sr, sc], -1)


def _swap_perm(width):
    half, h2 = width // 2, width // 4
    j = np.arange(width)
    return (j // half) * half + ((j % half) + h2) % half


def _swap_rows(x, width):
    h2 = width // 4
    p = [x[i * h2:(i + 1) * h2] for i in range(4)]
    return jnp.concatenate([p[1], p[0], p[3], p[2]], axis=0)


def _tables(seq, rotary):
    if rotary:
        c64, s64 = _rope_tables(seq, DIFF_HD)
        c32, s32 = _rope_tables(seq, MLA_ROPE)
    else:
        c64, s64 = jnp.ones((seq, DIFF_HD), F32), jnp.zeros((seq, DIFF_HD), F32)
        c32, s32 = jnp.ones((seq, MLA_ROPE), F32), jnp.zeros((seq, MLA_ROPE), F32)
    pad = jnp.zeros((seq, V7X_LANES - MLA_ROPE), F32)
    nat = jnp.concatenate([c64, c64, s64, s64, c32, pad, s32, pad], axis=-1)
    tr = jnp.concatenate([c32, s32, c64, s64], axis=-1).T
    return nat, tr


def _mod_kernel(c_ref, w_ref, b_ref, o_ref):
    c = c_ref[...]
    s = c * jax.nn.sigmoid(c)
    o_ref[...] = jnp.dot(s, w_ref[...], precision=lax.Precision.HIGHEST,
                         preferred_element_type=F32) + b_ref[...]


def _modulation(cc, w_mod, b_mod):
    rows, d = cc.shape
    n = w_mod.shape[1]
    return pl.pallas_call(
        _mod_kernel,
        grid=(n // d,),
        in_specs=[pl.BlockSpec((rows, d), lambda j: (0, 0)),
                  pl.BlockSpec((d, d), lambda j: (0, j)),
                  pl.BlockSpec((1, d), lambda j: (0, j))],
        out_specs=pl.BlockSpec((rows, d), lambda j: (0, j)),
        out_shape=jax.ShapeDtypeStruct((rows, n), F32),
        compiler_params=_params(("arbitrary",)),
        name="modulation",
    )(cc, w_mod, b_mod)


def _rms_rows(c, w):
    return c * lax.rsqrt(jnp.mean(c * c, axis=-1, keepdims=True) + RMS_EPS) * w


def _proj_kernel(x_ref, ctx_ref, scale_ref, shift_ref, scale_c_ref, shift_c_ref, wnat_ref, wt_ref, qn_ref,
                 kvn_ref, wuq_ref, wuk_ref, ekr_ref, wuv_ref, tabn_ref, tabt_ref,
                 ka_ref, vta_ref, kd_ref, vtd_ref, qta_ref, qtd_ref, gat_ref, gdt_ref, mt_ref, *, n_ctx_steps):
    n_kv, tk = vta_ref.shape[1], vta_ref.shape[-1]
    lanes = V7X_LANES
    step = pl.program_id(1)
    is_ctx = step < n_ctx_steps

    tok = jnp.where(is_ctx, ctx_ref[0], x_ref[0])
    scale = jnp.where(is_ctx, scale_c_ref[0], scale_ref[0])
    shift = jnp.where(is_ctx, shift_c_ref[0], shift_ref[0])
    h = (tok * (1.0 + scale) + shift).astype(BF16)
    pn = _dot(h, wnat_ref[...])

    ckv = _rms_rows(pn[:, _NAT_CKV[0]:_NAT_CKV[1]], kvn_ref[...]).astype(BF16)
    kr = (pn[:, _NAT_KR[0]:_NAT_KR[1]] * tabn_ref[:, 2 * lanes:3 * lanes]
          + pn[:, _NAT_KRSW[0]:_NAT_KRSW[1]] * tabn_ref[:, 3 * lanes:4 * lanes])
    ka = _dot(ckv, wuk_ref[...]) + _dot(kr.astype(BF16), ekr_ref[...])
    ka_ref[0] = ka.astype(BF16)
    vta = _nt_dot(wuv_ref[...], ckv).astype(BF16)
    for j in range(n_kv):
        vta_ref[0, j] = vta[:, j * tk:(j + 1) * tk]

    for hh in range(DIFF_HEADS):
        a = _NAT_DK[0] + hh * lanes
        b = _NAT_DKSW[0] + hh * lanes
        kd = pn[:, a:a + lanes] * tabn_ref[:, 0:lanes] + pn[:, b:b + lanes] * tabn_ref[:, lanes:2 * lanes]
        kd_ref[0, :, hh * lanes:(hh + 1) * lanes] = kd.astype(BF16)

    vtd = _nt_dot(wt_ref[_T_DV[0]:_T_DV[1], :], h).astype(BF16)
    for j in range(n_kv):
        vtd_ref[0, j] = vtd[:, j * tk:(j + 1) * tk]

    @pl.when(jnp.logical_not(is_ctx))
    def _():
        cq = _rms_rows(pn[:, _NAT_CQ[0]:_NAT_CQ[1]], qn_ref[...]).astype(BF16)
        qta = _nt_dot(wuq_ref[...], cq) * (MLA_SCALE * LOG2E)
        c32, s32 = tabt_ref[0:MLA_ROPE, :], tabt_ref[MLA_ROPE:2 * MLA_ROPE, :]
        for hh in range(MLA_HEADS):
            base = hh * lanes
            r = qta[base + MLA_NOPE:base + MLA_NOPE + MLA_ROPE]
            r = r * c32 + _swap_rows(r, MLA_ROPE) * s32
            head = jnp.concatenate([qta[base:base + MLA_NOPE], r, qta[base + MLA_NOPE + MLA_ROPE:base + lanes]],
                                   axis=0)
            qta_ref[0, hh, 0] = head.astype(BF16)

        qtd = _nt_dot(wt_ref[_T_DQ[0]:_T_DQ[1], :], h) * (DIFF_SCALE * LOG2E)
        c64 = tabt_ref[2 * MLA_ROPE:2 * MLA_ROPE + DIFF_HD, :]
        s64 = tabt_ref[2 * MLA_ROPE + DIFF_HD:2 * MLA_ROPE + 2 * DIFF_HD, :]
        for hh in range(DIFF_HEADS):
            halves = []
            for half in range(2):
                blk = qtd[(2 * hh + half) * DIFF_HD:(2 * hh + half + 1) * DIFF_HD]
                halves.append(blk * c64 + _swap_rows(blk, DIFF_HD) * s64)
            qtd_ref[0, hh, 0] = jnp.concatenate(halves, axis=0).astype(BF16)

        ga = _nt_dot(wt_ref[_T_GA[0]:_T_GA[1], :], h)
        gat_ref[0] = (ga * jax.nn.sigmoid(ga)).astype(BF16)
        gd = _nt_dot(wt_ref[_T_GD[0]:_T_GD[1], :], h)
        gdt_ref[0] = (gd * jax.nn.sigmoid(gd)).astype(BF16)
        mt_ref[0] = jax.nn.sigmoid(_nt_dot(wt_ref[_T_MERGE[0]:_T_MERGE[1], :], h)).astype(BF16)


def _projection(x, ctx, scale, shift, scale_c, shift_c, w, tabn, tabt):
    bsz, s, d = x.shape
    c = ctx.shape[1]
    ts, tk, tq = PROJ_TOKENS, KV_TILE, min(Q_TILE, s)
    assert s % ts == 0 and c % ts == 0 and ts % tk == 0 and tq % ts == 0 and s % tq == 0
    n_lat, n_ctx, n_kv, per_q = s // ts, c // ts, ts // tk, tq // ts
    n_total = s + c

    lat = lambda i: jnp.maximum(i - n_ctx, 0)
    kvb = lambda i: jnp.where(i < n_ctx, n_lat + i, i - n_ctx)
    consts = [w[name] for name in ("wnat", "wt", "qn", "kvn", "wuq", "wuk", "ekr", "wuv")]
    in_arrays = [x, ctx, scale, shift, scale_c, shift_c, *consts, tabn, tabt]
    in_specs = [pl.BlockSpec((1, ts, d), lambda b, i: (b, lat(i), 0)),
                pl.BlockSpec((1, ts, d), lambda b, i: (b, jnp.minimum(i, n_ctx - 1), 0)),
                pl.BlockSpec((1, 1, d), lambda b, i: (b, 0, 0)), pl.BlockSpec((1, 1, d), lambda b, i: (b, 0, 0)),
                pl.BlockSpec((1, 1, d), lambda b, i: (0, 0, 0)), pl.BlockSpec((1, 1, d), lambda b, i: (0, 0, 0)),
                *[_const_spec(a.shape) for a in consts],
                pl.BlockSpec((ts, tabn.shape[1]), lambda b, i: (kvb(i), 0)),
                pl.BlockSpec((tabt.shape[0], ts), lambda b, i: (0, lat(i)))]

    hp = MLA_HEADS * V7X_LANES
    feat = lambda rows: (jax.ShapeDtypeStruct((bsz, rows, s), BF16),
                         pl.BlockSpec((1, rows, ts), lambda b, i: (b, 0, lat(i))))
    qtile = lambda heads: (jax.ShapeDtypeStruct((bsz, heads, s // tq, V7X_LANES, tq), BF16),
                           pl.BlockSpec((1, heads, 1, V7X_LANES, ts),
                                        lambda b, i: (b, 0, lat(i) // per_q, 0, lat(i) % per_q)))
    outs = [(jax.ShapeDtypeStruct((bsz, n_total, hp), BF16),
             pl.BlockSpec((1, ts, hp), lambda b, i: (b, kvb(i), 0))),
            (jax.ShapeDtypeStruct((bsz, n_total // tk, MLA_WIDTH, tk), BF16),
             pl.BlockSpec((1, n_kv, MLA_WIDTH, tk), lambda b, i: (b, kvb(i), 0, 0))),
            (jax.ShapeDtypeStruct((bsz, n_total, DIFF_WIDTH), BF16),
             pl.BlockSpec((1, ts, DIFF_WIDTH), lambda b, i: (b, kvb(i), 0))),
            (jax.ShapeDtypeStruct((bsz, n_total // tk, DIFF_WIDTH, tk), BF16),
             pl.BlockSpec((1, n_kv, DIFF_WIDTH, tk), lambda b, i: (b, kvb(i), 0, 0))),
            qtile(MLA_HEADS), qtile(DIFF_HEADS),
            feat(MLA_WIDTH), feat(DIFF_WIDTH), feat(2 * D_MODEL)]
    return pl.pallas_call(
        functools.partial(_proj_kernel, n_ctx_steps=n_ctx),
        grid=(bsz, n_ctx + n_lat),
        in_specs=in_specs,
        out_specs=[o[1] for o in outs],
        out_shape=[o[0] for o in outs],
        compiler_params=_params(("parallel", "arbitrary")),
        name="projection",
    )(*in_arrays)


def _ones_rows(tk):
    rows = lax.broadcasted_iota(jnp.int32, (DENOM_ROWS, tk), 0)
    return jnp.where(rows == 0, 1.0, 0.0).astype(BF16)


def _softmax_stage(s, vts, ones, m_sc, acc_sc, first):
    tk = ones.shape[1]
    m_new = jnp.max(s, axis=0, keepdims=True)
    if not first:
        m_old = m_sc[...]
        m_new = jnp.maximum(m_old, m_new)
        alpha = jnp.exp2(m_old - m_new)
    p = jnp.exp2(s - m_new).astype(BF16)
    pv = None
    for i, vt in enumerate(vts):
        d = _dot(jnp.concatenate([vt, ones], axis=0), p[i * tk:(i + 1) * tk])
        pv = d if pv is None else pv + d
    acc_sc[...] = pv if first else alpha * acc_sc[...] + pv
    m_sc[...] = m_new


def _normalised(acc_sc, dv):
    acc = acc_sc[...]
    return acc[:dv] / acc[dv:dv + 1]


def _kv_group(n_tiles):
    return next(g for g in (3, 2, 1) if n_tiles % g == 0)


def _attention_pipeline(n_q, n_stage, scores, process, finish):
    scores(0, 0, 0)

    def body(qi, carry):
        nxt = jnp.minimum(qi + 1, n_q - 1)
        for j in range(n_stage):
            slot = j % 2
            last = j == n_stage - 1
            nslot = 0 if last else (j + 1) % 2
            if nslot != slot:
                scores(nxt if last else qi, 0 if last else j + 1, nslot)
                process(j, slot)
            else:
                process(j, slot)
                scores(nxt if last else qi, 0 if last else j + 1, nslot)
        finish(qi)
        return carry

    lax.fori_loop(0, n_q, body, 0)


def _mla_attn_kernel(q_ref, k_ref, vt_ref, o_ref, s_sc, m_sc, acc_sc):
    n_q = q_ref.shape[2]
    n_tiles, dv, tk = vt_ref.shape[1:]
    g = _kv_group(n_tiles)
    rows = g * tk
    ones = _ones_rows(tk)

    def scores(qi, j, slot):
        s_sc[slot] = _dot(k_ref[0, j * rows:(j + 1) * rows, :], q_ref[0, 0, qi])

    def process(j, slot):
        _softmax_stage(s_sc[slot], [vt_ref[0, j * g + i] for i in range(g)], ones, m_sc, acc_sc, j == 0)

    def finish(qi):
        o_ref[0, 0, qi] = _normalised(acc_sc, dv).astype(o_ref.dtype)

    _attention_pipeline(n_q, n_tiles // g, scores, process, finish)


def _diff_attn_kernel(q_ref, k_ref, vt_ref, lam_a_ref, lam_b_ref, subln_ref, o_ref, s_sc, m1, a1, m2, a2):
    n_q = q_ref.shape[2]
    n_tiles, dv, tk = vt_ref.shape[1:]
    g = _kv_group(n_tiles)
    rows = g * tk
    ones = _ones_rows(tk)
    zeros = jnp.zeros((DIFF_HD, q_ref.shape[-1]), q_ref.dtype)

    e = jnp.exp(jnp.sum(lam_a_ref[...] * lam_b_ref[...], axis=1, keepdims=True))
    sign = jnp.where(lax.broadcasted_iota(jnp.int32, e.shape, 0) == 0, 1.0, -1.0)
    lam = jnp.sum(e * sign, axis=0, keepdims=True) + LAM_INIT

    def scores(qi, j, slot):
        q = q_ref[0, 0, qi]
        k = k_ref[0, j * rows:(j + 1) * rows, :]
        s_sc[slot, 0] = _dot(k, jnp.concatenate([q[:DIFF_HD], zeros], axis=0))
        s_sc[slot, 1] = _dot(k, jnp.concatenate([zeros, q[DIFF_HD:]], axis=0))

    def process(j, slot):
        vts = [vt_ref[0, j * g + i] for i in range(g)]
        _softmax_stage(s_sc[slot, 0], vts, ones, m1, a1, j == 0)
        _softmax_stage(s_sc[slot, 1], vts, ones, m2, a2, j == 0)

    def finish(qi):
        o = _normalised(a1, dv) - lam * _normalised(a2, dv)
        o = o * lax.rsqrt(jnp.mean(o * o, axis=0, keepdims=True) + SUBLN_EPS)
        o_ref[0, 0, qi] = ((o * subln_ref[...]) * (1.0 - LAM_INIT)).astype(o_ref.dtype)

    _attention_pipeline(n_q, n_tiles // g, scores, process, finish)


def _attention(qt, k, vt, *, dv, extra=(), kernel_fn, n_softmax, name):
    bsz, heads, n_q, lanes, tq = qt.shape
    t = k.shape[1]
    n_tiles, tk = vt.shape[1], vt.shape[-1]
    assert n_tiles * tk == t and lanes == V7X_LANES
    in_specs = [pl.BlockSpec((1, 1, n_q, lanes, tq), lambda b, h: (b, h, 0, 0, 0)),
                pl.BlockSpec((1, t, lanes), lambda b, h: (b, 0, h)),
                pl.BlockSpec((1, n_tiles, dv, tk), lambda b, h: (b, 0, h, 0))]
    in_specs += [_const_spec(a.shape) for a in extra]
    rows = _kv_group(n_tiles) * tk
    s_shape = (2, rows, tq) if n_softmax == 1 else (2, n_softmax, rows, tq)
    scratch = [pltpu.VMEM(s_shape, F32)]
    for _ in range(n_softmax):
        scratch += [pltpu.VMEM((1, tq), F32), pltpu.VMEM((dv + DENOM_ROWS, tq), F32)]
    return pl.pallas_call(
        kernel_fn,
        grid=(bsz, heads),
        in_specs=in_specs,
        out_specs=pl.BlockSpec((1, 1, n_q, dv, tq), lambda b, h: (b, h, 0, 0, 0)),
        out_shape=jax.ShapeDtypeStruct((bsz, heads, n_q, dv, tq), BF16),
        scratch_shapes=scratch,
        compiler_params=_params(("parallel", "parallel")),
        name=name,
    )(qt, k, vt, *extra)


def _out_kernel(oa_ref, od_ref, ga_ref, gd_ref, m_ref, x_ref, gate_ref, woa_ref, wob_ref, wout_ref,
                g_ref, b_ref, o_ref):
    oa = jnp.concatenate([oa_ref[0, hh, 0] for hh in range(MLA_HEADS)], axis=0)
    od = jnp.concatenate([od_ref[0, hh, 0] for hh in range(DIFF_HEADS)], axis=0)
    ua = (oa.astype(F32) * ga_ref[0].astype(F32)).astype(BF16)
    ud = (od.astype(F32) * gd_ref[0].astype(F32)).astype(BF16)
    ya = _dot(woa_ref[...], ua)
    yd = _dot(wob_ref[...], ud)
    mix = m_ref[0, :D_MODEL, :].astype(F32) * ya + m_ref[0, D_MODEL:, :].astype(F32) * yd
    y = _tn_dot(mix.astype(BF16), wout_ref[...])
    z = ALPHA * x_ref[0] + gate_ref[0] * y
    mu = jnp.mean(z, axis=-1, keepdims=True)
    zc = z - mu
    var = jnp.mean(zc * zc, axis=-1, keepdims=True)
    o_ref[0] = zc * lax.rsqrt(var + LN_EPS) * g_ref[...] + b_ref[...]


def _output(oa, od, ga, gd, mt, x, gate, w):
    bsz, s, d = x.shape
    tf = oa.shape[-1]
    feat = lambda rows: pl.BlockSpec((1, rows, tf), lambda b, i: (b, 0, i))
    qtile = lambda a: pl.BlockSpec((1, a.shape[1], 1, a.shape[3], tf), lambda b, i: (b, 0, i, 0, 0))
    return pl.pallas_call(
        _out_kernel,
        grid=(bsz, s // tf),
        in_specs=[qtile(oa), qtile(od), feat(MLA_WIDTH), feat(DIFF_WIDTH), feat(2 * d),
                  pl.BlockSpec((1, tf, d), lambda b, i: (b, i, 0)),
                  pl.BlockSpec((1, 1, d), lambda b, i: (b, 0, 0)),
                  _const_spec(w["woa"].shape), _const_spec(w["wob"].shape), _const_spec(w["wout"].shape),
                  _const_spec(w["ln_g"].shape), _const_spec(w["ln_b"].shape)],
        out_specs=pl.BlockSpec((1, tf, d), lambda b, i: (b, i, 0)),
        out_shape=jax.ShapeDtypeStruct((bsz, s, d), F32),
        compiler_params=_params(("parallel", "parallel")),
        name="output",
    )(oa, od, ga, gd, mt, x, gate, w["woa"], w["wob"], w["wout"], w["ln_g"], w["ln_b"])


def _prep_weights(w_in, mla_q_norm, mla_kv_norm, w_uq, w_ukv, w_oa, w_ob, w_out, ln_g, ln_b):
    o = IN_OFFSETS
    cols = lambda i: w_in[:, o[i]:o[i + 1]]
    w_cq, w_ckv, w_kr, w_ga, w_dq, w_dk, w_dv, w_gd, w_mg = (cols(i) for i in range(9))
    lanes = V7X_LANES
    pad_kr = jnp.zeros((D_MODEL, lanes - MLA_ROPE), F32)
    dk_perm = np.concatenate([g * DIFF_HD + _swap_perm(DIFF_HD) for g in range(2 * DIFF_HEADS)])
    wnat = jnp.concatenate([w_cq, w_ckv, w_dk, w_dk[:, dk_perm], w_kr, pad_kr,
                            w_kr[:, _swap_perm(MLA_ROPE)], pad_kr], axis=1)
    assert wnat.shape[1] == _NAT_COLS
    wt = jnp.concatenate([w_dq, w_dv, w_ga, w_gd, w_mg], axis=1).T
    assert wt.shape[0] == _T_ROWS

    uq = w_uq.reshape(MLA_Q_LORA, MLA_HEADS, MLA_NOPE + MLA_ROPE)
    uq = jnp.pad(uq, ((0, 0), (0, 0), (0, lanes - MLA_NOPE - MLA_ROPE)))
    wuq = uq.reshape(MLA_Q_LORA, MLA_HEADS * lanes).T
    ukv = w_ukv.reshape(MLA_KV_LORA, MLA_HEADS, MLA_NOPE + MLA_V)
    wuk = jnp.pad(ukv[:, :, :MLA_NOPE], ((0, 0), (0, 0), (0, lanes - MLA_NOPE)))
    wuk = wuk.reshape(MLA_KV_LORA, MLA_HEADS * lanes)
    wuv = ukv[:, :, MLA_NOPE:].reshape(MLA_KV_LORA, MLA_WIDTH).T
    ekr = np.zeros((lanes, MLA_HEADS * lanes), np.float32)
    for hh in range(MLA_HEADS):
        ekr[np.arange(MLA_ROPE), hh * lanes + MLA_NOPE + np.arange(MLA_ROPE)] = 1.0
    return {
        "wnat": wnat.astype(BF16), "wt": wt.astype(BF16),
        "qn": mla_q_norm[None, :], "kvn": mla_kv_norm[None, :],
        "wuq": wuq.astype(BF16), "wuk": wuk.astype(BF16), "wuv": wuv.astype(BF16),
        "ekr": jnp.asarray(ekr, BF16),
        "woa": w_oa.T.astype(BF16), "wob": w_ob.T.astype(BF16), "wout": w_out.astype(BF16),
        "ln_g": ln_g[None, :], "ln_b": ln_b[None, :],
    }


def kernel(x, c, ctx, c_ctx, w_mod, b_mod, w_in, mla_q_norm, mla_kv_norm, w_uq, w_ukv, diff_lambda,
           diff_subln, w_oa, w_ob, w_out, ln_g, ln_b):
    bsz, s, d = x.shape
    n_ctx = ctx.shape[1]
    assert w_mod.shape[0] == DEPTH == 1 and d == D_MODEL and s % GRID_W == 0 and bsz + 1 <= MOD_ROWS

    cc = jnp.concatenate([c, c_ctx[None, :], jnp.zeros((MOD_ROWS - bsz - 1, d), F32)], axis=0)
    mod = _modulation(cc, w_mod[0], b_mod[0][None, :])
    shift, scale, gate = (mod[:bsz, i * d:(i + 1) * d][:, None, :] for i in range(3))
    shift_c, scale_c = (mod[bsz:bsz + 1, i * d:(i + 1) * d][:, None, :] for i in range(2))

    w = _prep_weights(w_in[0], mla_q_norm[0], mla_kv_norm[0], w_uq[0], w_ukv[0], w_oa[0], w_ob[0], w_out[0],
                      ln_g[0], ln_b[0])
    tabn, tabt = _tables(s, rotary=True)
    tabn_c, _ = _tables(n_ctx, rotary=False)
    tabn = jnp.concatenate([tabn, tabn_c], axis=0)

    ka, vta, kd, vtd, qta, qtd, gat, gdt, mt = _projection(x, ctx, scale, shift, scale_c, shift_c, w, tabn, tabt)

    oa = _attention(qta, ka, vta, dv=MLA_V, kernel_fn=_mla_attn_kernel, n_softmax=1, name="attention_mla")
    lam_a = diff_lambda[0][0::2]
    lam_b = diff_lambda[0][1::2]
    od = _attention(qtd, kd, vtd, dv=2 * DIFF_HD, extra=(lam_a, lam_b, diff_subln[0][:, None]),
                    kernel_fn=_diff_attn_kernel, n_softmax=2, name="attention_diff")
    return _output(oa, od, gat, gdt, mt, x, gate, w)
```

```python
import functools
import math

import numpy as np
import jax
import jax.numpy as jnp
from jax import lax
from jax.experimental import pallas as pl
from jax.experimental.pallas import tpu as pltpu

F32 = jnp.float32
BF16 = jnp.bfloat16

D_MODEL = 1024
GRID_W = 64
ROPE_THETA = 10000.0
LN_EPS = 1e-5
RMS_EPS = 1e-6
SUBLN_EPS = 1e-5
MLA_HEADS = 8
MLA_NOPE = 64
MLA_ROPE = 32
MLA_V = 64
MLA_Q_LORA = 384
MLA_KV_LORA = 256
MLA_WIDTH = MLA_HEADS * MLA_V
MLA_SCALE = (MLA_NOPE + MLA_ROPE) ** -0.5
DIFF_HEADS = 4
DIFF_HD = 64
DIFF_WIDTH = DIFF_HEADS * 2 * DIFF_HD
DIFF_SCALE = DIFF_HD ** -0.5
DEPTH = 1
ALPHA = (2 * DEPTH) ** 0.25
LAM_INIT = 0.8 - 0.6 * math.exp(-0.3 * 0)
IN_SPLITS = (MLA_Q_LORA, MLA_KV_LORA, MLA_ROPE, MLA_WIDTH, DIFF_WIDTH, DIFF_WIDTH, DIFF_WIDTH, DIFF_WIDTH,
             2 * D_MODEL)
IN_OFFSETS = tuple(int(o) for o in np.cumsum((0,) + IN_SPLITS))
LOG2E = math.log2(math.e)

V7X_LANES = 128
V7X_VMEM_LIMIT_BYTES = 56 * 2**20
MOD_ROWS = 16
DENOM_ROWS = 16

PROJ_TOKENS = 256
KV_TILE = 256
MLA_Q_TILE = 512
DIFF_Q_TILE = 256
SCORE_LOOKAHEAD = 1
OUT_CHUNK = 256

_NAT_CQ = (0, 384)
_NAT_CKV = (384, 640)
_NAT_DK = (640, 1152)
_NAT_KR = (1152, 1280)
_NAT_COLS = 1280
_T_DQ = (0, 512)
_T_DV = (512, 1024)
_T_GA = (1024, 1536)
_T_GD = (1536, 2048)
_T_MERGE = (2048, 4096)
_T_ROWS = 4096


def _nt_dot(a, b):
    return lax.dot_general(a, b, (((1,), (1,)), ((), ())), preferred_element_type=F32)


def _tn_dot(a, b):
    return lax.dot_general(a, b, (((0,), (0,)), ((), ())), preferred_element_type=F32)


def _dot(a, b):
    return jnp.dot(a, b, preferred_element_type=F32)


def _const_spec(shape):
    nd = len(shape)
    return pl.BlockSpec(shape, lambda *_: (0,) * nd)


def _params(semantics):
    return pltpu.CompilerParams(dimension_semantics=semantics, vmem_limit_bytes=V7X_VMEM_LIMIT_BYTES)


def _rope_tables(seq, width):
    h2 = width // 4
    pos = jnp.arange(seq, dtype=jnp.int32)
    row = (pos // GRID_W).astype(F32)
    col = (pos % GRID_W).astype(F32)
    inv = ROPE_THETA ** (-jnp.arange(h2, dtype=F32) / h2)

    def part(p):
        ang = p[:, None] * inv[None, :]
        c, s = jnp.cos(ang), jnp.sin(ang)
        return jnp.concatenate([c, c], -1), jnp.concatenate([-s, s], -1)

    cr, sr = part(row)
    cc, sc = part(col)
    return jnp.concatenate([cr, cc], -1), jnp.concatenate([sr, sc], -1)


def _swap_lanes(x, width):
    h2 = width // 4
    lanes = x.shape[-1]
    lane = lax.broadcasted_iota(jnp.int32, x.shape, x.ndim - 1)
    return jnp.where((lane & h2) == 0, pltpu.roll(x, lanes - h2, x.ndim - 1), pltpu.roll(x, h2, x.ndim - 1))


def _swap_rows(x, width):
    h2 = width // 4
    p = [x[i * h2:(i + 1) * h2] for i in range(4)]
    return jnp.concatenate([p[1], p[0], p[3], p[2]], axis=0)


def _tables(seq, rotary):
    if rotary:
        c64, s64 = _rope_tables(seq, DIFF_HD)
        c32, s32 = _rope_tables(seq, MLA_ROPE)
    else:
        c64, s64 = jnp.ones((seq, DIFF_HD), F32), jnp.zeros((seq, DIFF_HD), F32)
        c32, s32 = jnp.ones((seq, MLA_ROPE), F32), jnp.zeros((seq, MLA_ROPE), F32)
    pad = jnp.zeros((seq, V7X_LANES - MLA_ROPE), F32)
    nat = jnp.concatenate([c64, c64, s64, s64, c32, pad, s32, pad], axis=-1)
    tr = jnp.concatenate([c32, s32, c64, s64], axis=-1).T
    return nat, tr


def _mod_kernel(c_ref, w_ref, b_ref, o_ref):
    c = c_ref[...]
    s = c * jax.nn.sigmoid(c)
    o_ref[...] = jnp.dot(s, w_ref[...], precision=lax.Precision.HIGHEST,
                         preferred_element_type=F32) + b_ref[...]


def _modulation(cc, w_mod, b_mod):
    rows, d = cc.shape
    n = w_mod.shape[1]
    return pl.pallas_call(
        _mod_kernel,
        grid=(n // d,),
        in_specs=[pl.BlockSpec((rows, d), lambda j: (0, 0)),
                  pl.BlockSpec((d, d), lambda j: (0, j)),
                  pl.BlockSpec((1, d), lambda j: (0, j))],
        out_specs=pl.BlockSpec((rows, d), lambda j: (0, j)),
        out_shape=jax.ShapeDtypeStruct((rows, n), F32),
        compiler_params=_params(("arbitrary",)),
        name="modulation",
    )(cc, w_mod, b_mod)


def _rms_rows(c, w):
    return c * lax.rsqrt(jnp.mean(c * c, axis=-1, keepdims=True) + RMS_EPS) * w


def _proj_kernel(x_ref, ctx_ref, scale_ref, shift_ref, scale_c_ref, shift_c_ref, wnat_ref, wt_ref, qn_ref,
                 kvn_ref, wuq_ref, wuk_ref, ekr_ref, wuv_ref, tabn_ref, tabt_ref,
                 ka_ref, vta_ref, kd_ref, vtd_ref, qta_ref, qtd_ref, gat_ref, gdt_ref, mt_ref, *, n_ctx_steps):
    n_kv, tk = vta_ref.shape[1], vta_ref.shape[-1]
    lanes = V7X_LANES
    step = pl.program_id(1)
    is_ctx = step < n_ctx_steps

    tok = jnp.where(is_ctx, ctx_ref[0], x_ref[0])
    scale = jnp.where(is_ctx, scale_c_ref[0], scale_ref[0])
    shift = jnp.where(is_ctx, shift_c_ref[0], shift_ref[0])
    h = (tok * (1.0 + scale) + shift).astype(BF16)

    mt_ref[0] = jax.nn.sigmoid(_nt_dot(wt_ref[_T_MERGE[0]:_T_MERGE[1], :], h)).astype(BF16)
    ga = _nt_dot(wt_ref[_T_GA[0]:_T_GA[1], :], h)
    gat_ref[0] = (ga * jax.nn.sigmoid(ga)).astype(BF16)
    gd = _nt_dot(wt_ref[_T_GD[0]:_T_GD[1], :], h)
    gdt_ref[0] = (gd * jax.nn.sigmoid(gd)).astype(BF16)

    pn = _dot(h, wnat_ref[...])

    ckv = _rms_rows(pn[:, _NAT_CKV[0]:_NAT_CKV[1]], kvn_ref[...]).astype(BF16)
    kr = pn[:, _NAT_KR[0]:_NAT_KR[1]]
    kr = kr * tabn_ref[:, 2 * lanes:3 * lanes] + _swap_lanes(kr, MLA_ROPE) * tabn_ref[:, 3 * lanes:4 * lanes]
    ka = _dot(ckv, wuk_ref[...]) + _dot(kr.astype(BF16), ekr_ref[...])
    ka_ref[0] = ka.astype(BF16)
    vta = _nt_dot(wuv_ref[...], ckv).astype(BF16)
    for j in range(n_kv):
        vta_ref[0, j] = vta[:, j * tk:(j + 1) * tk]

    for hh in range(DIFF_HEADS):
        a = _NAT_DK[0] + hh * lanes
        kd = pn[:, a:a + lanes]
        kd = kd * tabn_ref[:, 0:lanes] + _swap_lanes(kd, DIFF_HD) * tabn_ref[:, lanes:2 * lanes]
        kd_ref[0, :, hh * lanes:(hh + 1) * lanes] = kd.astype(BF16)

    vtd = _nt_dot(wt_ref[_T_DV[0]:_T_DV[1], :], h).astype(BF16)
    for j in range(n_kv):
        vtd_ref[0, j] = vtd[:, j * tk:(j + 1) * tk]


    cq = _rms_rows(pn[:, _NAT_CQ[0]:_NAT_CQ[1]], qn_ref[...]).astype(BF16)
    qta = _nt_dot(wuq_ref[...], cq) * (MLA_SCALE * LOG2E)
    c32, s32 = tabt_ref[0:MLA_ROPE, :], tabt_ref[MLA_ROPE:2 * MLA_ROPE, :]
    for hh in range(MLA_HEADS):
        base = hh * lanes
        r = qta[base + MLA_NOPE:base + MLA_NOPE + MLA_ROPE]
        r = r * c32 + _swap_rows(r, MLA_ROPE) * s32
        head = jnp.concatenate([qta[base:base + MLA_NOPE], r, qta[base + MLA_NOPE + MLA_ROPE:base + lanes]],
                               axis=0)
        qta_ref[0, hh, 0] = head.astype(BF16)

    qtd = _nt_dot(wt_ref[_T_DQ[0]:_T_DQ[1], :], h) * (DIFF_SCALE * LOG2E)
    c64 = tabt_ref[2 * MLA_ROPE:2 * MLA_ROPE + DIFF_HD, :]
    s64 = tabt_ref[2 * MLA_ROPE + DIFF_HD:2 * MLA_ROPE + 2 * DIFF_HD, :]
    for hh in range(DIFF_HEADS):
        halves = []
        for half in range(2):
            blk = qtd[(2 * hh + half) * DIFF_HD:(2 * hh + half + 1) * DIFF_HD]
            halves.append(blk * c64 + _swap_rows(blk, DIFF_HD) * s64)
        qtd_ref[0, hh, 0] = jnp.concatenate(halves, axis=0).astype(BF16)


def _projection(x, ctx, scale, shift, scale_c, shift_c, w, tabn, tabt):
    bsz, s, d = x.shape
    c = ctx.shape[1]
    ts, tk = PROJ_TOKENS, KV_TILE
    assert s % ts == 0 and c % ts == 0 and ts % tk == 0
    n_lat, n_ctx, n_kv = s // ts, c // ts, ts // tk
    n_total = s + c

    lat = lambda i: jnp.maximum(i - n_ctx, 0)
    kvb = lambda i: jnp.where(i < n_ctx, n_lat + i, i - n_ctx)
    consts = [w[name] for name in ("wnat", "wt", "qn", "kvn", "wuq", "wuk", "ekr", "wuv")]
    in_arrays = [x, ctx, scale, shift, scale_c, shift_c, *consts, tabn, tabt]
    in_specs = [pl.BlockSpec((1, ts, d), lambda b, i: (b, lat(i), 0)),
                pl.BlockSpec((1, ts, d), lambda b, i: (b, jnp.minimum(i, n_ctx - 1), 0)),
                pl.BlockSpec((1, 1, d), lambda b, i: (b, 0, 0)), pl.BlockSpec((1, 1, d), lambda b, i: (b, 0, 0)),
                pl.BlockSpec((1, 1, d), lambda b, i: (0, 0, 0)), pl.BlockSpec((1, 1, d), lambda b, i: (0, 0, 0)),
                *[_const_spec(a.shape) for a in consts],
                pl.BlockSpec((ts, tabn.shape[1]), lambda b, i: (kvb(i), 0)),
                pl.BlockSpec((tabt.shape[0], ts), lambda b, i: (0, lat(i)))]

    hp = MLA_HEADS * V7X_LANES
    feat = lambda rows: (jax.ShapeDtypeStruct((bsz, rows, s), BF16),
                         pl.BlockSpec((1, rows, ts), lambda b, i: (b, 0, lat(i))))
    def qtile(heads, q_tile):
        tq = min(q_tile, s)
        assert tq % ts == 0 and s % tq == 0
        per_q = tq // ts
        return (jax.ShapeDtypeStruct((bsz, heads, s // tq, V7X_LANES, tq), BF16),
                pl.BlockSpec((1, heads, 1, V7X_LANES, ts), lambda b, i: (b, 0, lat(i) // per_q, 0, lat(i) % per_q)))

    outs = [(jax.ShapeDtypeStruct((bsz, n_total, hp), BF16),
             pl.BlockSpec((1, ts, hp), lambda b, i: (b, kvb(i), 0))),
            (jax.ShapeDtypeStruct((bsz, n_total // tk, MLA_WIDTH, tk), BF16),
             pl.BlockSpec((1, n_kv, MLA_WIDTH, tk), lambda b, i: (b, kvb(i), 0, 0))),
            (jax.ShapeDtypeStruct((bsz, n_total, DIFF_WIDTH), BF16),
             pl.BlockSpec((1, ts, DIFF_WIDTH), lambda b, i: (b, kvb(i), 0))),
            (jax.ShapeDtypeStruct((bsz, n_total // tk, DIFF_WIDTH, tk), BF16),
             pl.BlockSpec((1, n_kv, DIFF_WIDTH, tk), lambda b, i: (b, kvb(i), 0, 0))),
            qtile(MLA_HEADS, MLA_Q_TILE), qtile(DIFF_HEADS, DIFF_Q_TILE),
            feat(MLA_WIDTH), feat(DIFF_WIDTH), feat(2 * D_MODEL)]
    return pl.pallas_call(
        functools.partial(_proj_kernel, n_ctx_steps=n_ctx),
        grid=(bsz, n_ctx + n_lat),
        in_specs=in_specs,
        out_specs=[o[1] for o in outs],
        out_shape=[o[0] for o in outs],
        compiler_params=_params(("parallel", "arbitrary")),
        name="projection",
    )(*in_arrays)


def _ones_rows(tk):
    rows = lax.broadcasted_iota(jnp.int32, (DENOM_ROWS, tk), 0)
    return jnp.where(rows == 0, 1.0, 0.0).astype(BF16)


def _softmax_stage(s, vts, ones, m_sc, acc_sc, first):
    m_new = jnp.max(s, axis=0, keepdims=True)
    if not first:
        m_old = m_sc[...]
        m_new = jnp.maximum(m_old, m_new)
        alpha = jnp.exp2(m_old - m_new)
    p = jnp.exp2(s - m_new).astype(BF16)
    vt_all = jnp.concatenate([jnp.concatenate([vt, ones], axis=0) for vt in vts], axis=1)
    pv = _dot(vt_all, p)
    acc_sc[...] = pv if first else alpha * acc_sc[...] + pv
    m_sc[...] = m_new


def _normalised(acc_sc, dv):
    acc = acc_sc[...]
    return acc[:dv] / acc[dv:dv + 1]


def _kv_group(n_tiles):
    return next(g for g in (3, 2, 1) if n_tiles % g == 0)


def _lookahead(n_stage):
    for ahead in range(min(SCORE_LOOKAHEAD, n_stage), 1, -1):
        slots = ahead + 1
        wrapped = [(j, j + ahead - n_stage) for j in range(n_stage) if j + ahead >= n_stage]
        if all((tj - pending) % slots != 0 for j, tj in wrapped for pending in range(j + 1, n_stage)):
            return ahead
    return 1


def _attention_pipeline(n_q, n_stage, scores, process, finish):
    ahead = _lookahead(n_stage)
    slots = ahead + 1
    for j in range(ahead):
        scores(0, j, j % slots)

    def body(qi, carry):
        nxt = jnp.minimum(qi + 1, n_q - 1)
        for j in range(n_stage):
            slot = j % slots
            t = j + ahead
            tq_idx, tj = (qi, t) if t < n_stage else (nxt, t - n_stage)
            if tj % slots != slot:
                scores(tq_idx, tj, tj % slots)
                process(j, slot)
            else:
                process(j, slot)
                scores(tq_idx, tj, tj % slots)
        finish(qi)
        return carry

    lax.fori_loop(0, n_q, body, 0)


def _mla_attn_kernel(q_ref, k_ref, vt_ref, o_ref, s_sc, m_sc, acc_sc):
    n_q = q_ref.shape[2]
    n_tiles, dv, tk = vt_ref.shape[1:]
    g = _kv_group(n_tiles)
    rows = g * tk
    ones = _ones_rows(tk)

    def scores(qi, j, slot):
        s_sc[slot, 0] = _dot(k_ref[0, j * rows:(j + 1) * rows, :], q_ref[0, 0, qi])

    def process(j, slot):
        _softmax_stage(s_sc[slot, 0], [vt_ref[0, j * g + i] for i in range(g)], ones, m_sc, acc_sc, j == 0)

    def finish(qi):
        o_ref[0, 0, qi] = _normalised(acc_sc, dv).astype(o_ref.dtype)

    _attention_pipeline(n_q, n_tiles // g, scores, process, finish)


def _diff_attn_kernel(q_ref, k_ref, vt_ref, lam_a_ref, lam_b_ref, subln_ref, o_ref, s_sc, m1, a1, m2, a2):
    n_q = q_ref.shape[2]
    n_tiles, dv, tk = vt_ref.shape[1:]
    g = _kv_group(n_tiles)
    rows = g * tk
    ones = _ones_rows(tk)
    zeros = jnp.zeros((DIFF_HD, q_ref.shape[-1]), q_ref.dtype)

    e = jnp.exp(jnp.sum(lam_a_ref[...] * lam_b_ref[...], axis=1, keepdims=True))
    sign = jnp.where(lax.broadcasted_iota(jnp.int32, e.shape, 0) == 0, 1.0, -1.0)
    lam = jnp.sum(e * sign, axis=0, keepdims=True) + LAM_INIT

    def scores(qi, j, slot):
        q = q_ref[0, 0, qi]
        k = k_ref[0, j * rows:(j + 1) * rows, :]
        s_sc[slot, 0] = _dot(k, jnp.concatenate([q[:DIFF_HD], zeros], axis=0))
        s_sc[slot, 1] = _dot(k, jnp.concatenate([zeros, q[DIFF_HD:]], axis=0))

    def process(j, slot):
        vts = [vt_ref[0, j * g + i] for i in range(g)]
        _softmax_stage(s_sc[slot, 0], vts, ones, m1, a1, j == 0)
        _softmax_stage(s_sc[slot, 1], vts, ones, m2, a2, j == 0)

    def finish(qi):
        o = _normalised(a1, dv) - lam * _normalised(a2, dv)
        o = o * lax.rsqrt(jnp.mean(o * o, axis=0, keepdims=True) + SUBLN_EPS)
        o_ref[0, 0, qi] = ((o * subln_ref[...]) * (1.0 - LAM_INIT)).astype(o_ref.dtype)

    _attention_pipeline(n_q, n_tiles // g, scores, process, finish)


def _attention(qt, k, vt, *, dv, extra=(), kernel_fn, n_softmax, name):
    bsz, heads, n_q, lanes, tq = qt.shape
    t = k.shape[1]
    n_tiles, tk = vt.shape[1], vt.shape[-1]
    assert n_tiles * tk == t and lanes == V7X_LANES
    in_specs = [pl.BlockSpec((1, 1, n_q, lanes, tq), lambda b, h: (b, h, 0, 0, 0)),
                pl.BlockSpec((1, t, lanes), lambda b, h: (b, 0, h)),
                pl.BlockSpec((1, n_tiles, dv, tk), lambda b, h: (b, 0, h, 0))]
    in_specs += [_const_spec(a.shape) for a in extra]
    rows = _kv_group(n_tiles) * tk
    slots = _lookahead(n_tiles // _kv_group(n_tiles)) + 1
    scratch = [pltpu.VMEM((slots, n_softmax, rows, tq), F32)]
    for _ in range(n_softmax):
        scratch += [pltpu.VMEM((1, tq), F32), pltpu.VMEM((dv + DENOM_ROWS, tq), F32)]
    return pl.pallas_call(
        kernel_fn,
        grid=(bsz, heads),
        in_specs=in_specs,
        out_specs=pl.BlockSpec((1, 1, n_q, dv, tq), lambda b, h: (b, h, 0, 0, 0)),
        out_shape=jax.ShapeDtypeStruct((bsz, heads, n_q, dv, tq), BF16),
        scratch_shapes=scratch,
        compiler_params=_params(("parallel", "parallel")),
        name=name,
    )(qt, k, vt, *extra)


def _out_kernel(oa_ref, od_ref, ga_ref, gd_ref, m_ref, x_ref, gate_ref, woa_ref, wob_ref, wout_ref,
                g_ref, b_ref, o_ref):
    tf = x_ref.shape[1]
    chunk = min(OUT_CHUNK, tf)

    def heads_major(ref, c0):
        tq = ref.shape[-1]
        t, off = c0 // tq, c0 % tq
        return jnp.concatenate([ref[0, hh, t, :, off:off + chunk] for hh in range(ref.shape[1])], axis=0)

    for c0 in range(0, tf, chunk):
        sl = slice(c0, c0 + chunk)
        ua = (heads_major(oa_ref, c0).astype(F32) * ga_ref[0, :, sl].astype(F32)).astype(BF16)
        ud = (heads_major(od_ref, c0).astype(F32) * gd_ref[0, :, sl].astype(F32)).astype(BF16)
        ya = _dot(woa_ref[...], ua)
        yd = _dot(wob_ref[...], ud)
        mix = m_ref[0, :D_MODEL, sl].astype(F32) * ya + m_ref[0, D_MODEL:, sl].astype(F32) * yd
        y = _tn_dot(mix.astype(BF16), wout_ref[...])
        z = ALPHA * x_ref[0, sl, :] + gate_ref[0] * y
        mu = jnp.mean(z, axis=-1, keepdims=True)
        zc = z - mu
        var = jnp.mean(zc * zc, axis=-1, keepdims=True)
        o_ref[0, sl, :] = zc * lax.rsqrt(var + LN_EPS) * g_ref[...] + b_ref[...]


def _output(oa, od, ga, gd, mt, x, gate, w):
    bsz, s, d = x.shape
    tf = max(oa.shape[-1], od.shape[-1])
    feat = lambda rows: pl.BlockSpec((1, rows, tf), lambda b, i: (b, 0, i))
    qtile = lambda a: pl.BlockSpec((1, a.shape[1], tf // a.shape[-1], a.shape[3], a.shape[-1]),
                                   lambda b, i: (b, 0, i, 0, 0))
    return pl.pallas_call(
        _out_kernel,
        grid=(bsz, s // tf),
        in_specs=[qtile(oa), qtile(od), feat(MLA_WIDTH), feat(DIFF_WIDTH), feat(2 * d),
                  pl.BlockSpec((1, tf, d), lambda b, i: (b, i, 0)),
                  pl.BlockSpec((1, 1, d), lambda b, i: (b, 0, 0)),
                  _const_spec(w["woa"].shape), _const_spec(w["wob"].shape), _const_spec(w["wout"].shape),
                  _const_spec(w["ln_g"].shape), _const_spec(w["ln_b"].shape)],
        out_specs=pl.BlockSpec((1, tf, d), lambda b, i: (b, i, 0)),
        out_shape=jax.ShapeDtypeStruct((bsz, s, d), F32),
        compiler_params=_params(("parallel", "parallel")),
        name="output",
    )(oa, od, ga, gd, mt, x, gate, w["woa"], w["wob"], w["wout"], w["ln_g"], w["ln_b"])


def _prep_weights(w_in, mla_q_norm, mla_kv_norm, w_uq, w_ukv, w_oa, w_ob, w_out, ln_g, ln_b):
    o = IN_OFFSETS
    cols = lambda i: w_in[:, o[i]:o[i + 1]]
    w_cq, w_ckv, w_kr, w_ga, w_dq, w_dk, w_dv, w_gd, w_mg = (cols(i) for i in range(9))
    lanes = V7X_LANES
    pad_kr = jnp.zeros((D_MODEL, lanes - MLA_ROPE), F32)
    wnat = jnp.concatenate([w_cq, w_ckv, w_dk, w_kr, pad_kr], axis=1)
    assert wnat.shape[1] == _NAT_COLS
    wt = jnp.concatenate([w_dq, w_dv, w_ga, w_gd, w_mg], axis=1).T
    assert wt.shape[0] == _T_ROWS

    uq = w_uq.reshape(MLA_Q_LORA, MLA_HEADS, MLA_NOPE + MLA_ROPE)
    uq = jnp.pad(uq, ((0, 0), (0, 0), (0, lanes - MLA_NOPE - MLA_ROPE)))
    wuq = uq.reshape(MLA_Q_LORA, MLA_HEADS * lanes).T
    ukv = w_ukv.reshape(MLA_KV_LORA, MLA_HEADS, MLA_NOPE + MLA_V)
    wuk = jnp.pad(ukv[:, :, :MLA_NOPE], ((0, 0), (0, 0), (0, lanes - MLA_NOPE)))
    wuk = wuk.reshape(MLA_KV_LORA, MLA_HEADS * lanes)
    wuv = ukv[:, :, MLA_NOPE:].reshape(MLA_KV_LORA, MLA_WIDTH).T
    ekr = np.zeros((lanes, MLA_HEADS * lanes), np.float32)
    for hh in range(MLA_HEADS):
        ekr[np.arange(MLA_ROPE), hh * lanes + MLA_NOPE + np.arange(MLA_ROPE)] = 1.0
    return {
        "wnat": wnat.astype(BF16), "wt": wt.astype(BF16),
        "qn": mla_q_norm[None, :], "kvn": mla_kv_norm[None, :],
        "wuq": wuq.astype(BF16), "wuk": wuk.astype(BF16), "wuv": wuv.astype(BF16),
        "ekr": jnp.asarray(ekr, BF16),
        "woa": w_oa.T.astype(BF16), "wob": w_ob.T.astype(BF16), "wout": w_out.astype(BF16),
        "ln_g": ln_g[None, :], "ln_b": ln_b[None, :],
    }


def kernel(x, c, ctx, c_ctx, w_mod, b_mod, w_in, mla_q_norm, mla_kv_norm, w_uq, w_ukv, diff_lambda,
           diff_subln, w_oa, w_ob, w_out, ln_g, ln_b):
    bsz, s, d = x.shape
    n_ctx = ctx.shape[1]
    assert w_mod.shape[0] == DEPTH == 1 and d == D_MODEL and s % GRID_W == 0 and bsz + 1 <= MOD_ROWS

    cc = jnp.concatenate([c, c_ctx[None, :], jnp.zeros((MOD_ROWS - bsz - 1, d), F32)], axis=0)
    mod = _modulation(cc, w_mod[0], b_mod[0][None, :])
    shift, scale, gate = (mod[:bsz, i * d:(i + 1) * d][:, None, :] for i in range(3))
    shift_c, scale_c = (mod[bsz:bsz + 1, i * d:(i + 1) * d][:, None, :] for i in range(2))

    w = _prep_weights(w_in[0], mla_q_norm[0], mla_kv_norm[0], w_uq[0], w_ukv[0], w_oa[0], w_ob[0], w_out[0],
                      ln_g[0], ln_b[0])
    tabn, tabt = _tables(s, rotary=True)
    tabn_c, _ = _tables(n_ctx, rotary=False)
    tabn = jnp.concatenate([tabn, tabn_c], axis=0)

    ka, vta, kd, vtd, qta, qtd, gat, gdt, mt = _projection(x, ctx, scale, shift, scale_c, shift_c, w, tabn, tabt)

    oa = _attention(qta, ka, vta, dv=MLA_V, kernel_fn=_mla_attn_kernel, n_softmax=1, name="attention_mla")
    lam_a = diff_lambda[0][0::2]
    lam_b = diff_lambda[0][1::2]
    od = _attention(qtd, kd, vtd, dv=2 * DIFF_HD, extra=(lam_a, lam_b, diff_subln[0][:, None]),
                    kernel_fn=_diff_attn_kernel, n_softmax=2, name="attention_diff")
    return _output(oa, od, gat, gdt, mt, x, gate, w)
```

```python
import functools
import math

import numpy as np
import jax
import jax.numpy as jnp
from jax import lax
from jax.experimental import pallas as pl
from jax.experimental.pallas import tpu as pltpu

F32 = jnp.float32
BF16 = jnp.bfloat16

D_MODEL = 1024
GRID_W = 64
ROPE_THETA = 10000.0
LN_EPS = 1e-5
RMS_EPS = 1e-6
SUBLN_EPS = 1e-5
MLA_HEADS = 8
MLA_NOPE = 64
MLA_ROPE = 32
MLA_V = 64
MLA_Q_LORA = 384
MLA_KV_LORA = 256
MLA_WIDTH = MLA_HEADS * MLA_V
MLA_SCALE = (MLA_NOPE + MLA_ROPE) ** -0.5
DIFF_HEADS = 4
DIFF_HD = 64
DIFF_WIDTH = DIFF_HEADS * 2 * DIFF_HD
DIFF_SCALE = DIFF_HD ** -0.5
DEPTH = 1
ALPHA = (2 * DEPTH) ** 0.25
LAM_INIT = 0.8 - 0.6 * math.exp(-0.3 * 0)
IN_SPLITS = (MLA_Q_LORA, MLA_KV_LORA, MLA_ROPE, MLA_WIDTH, DIFF_WIDTH, DIFF_WIDTH, DIFF_WIDTH, DIFF_WIDTH,
             2 * D_MODEL)
IN_OFFSETS = tuple(int(o) for o in np.cumsum((0,) + IN_SPLITS))
LOG2E = math.log2(math.e)

V7X_LANES = 128
V7X_VMEM_LIMIT_BYTES = 56 * 2**20
MOD_ROWS = 16
DENOM_ROWS = 16

PROJ_TOKENS = 256
KV_TILE = 256
MLA_Q_TILE = 512
DIFF_Q_TILE = 256
SCORE_SLOTS = 2
OUT_CHUNK = 256

_NAT_CQ = (0, 384)
_NAT_CKV = (384, 640)
_NAT_DK = (640, 1152)
_NAT_KR = (1152, 1280)
_NAT_COLS = 1280
_T_DQ = (0, 512)
_T_DV = (512, 1024)
_T_GA = (1024, 1536)
_T_GD = (1536, 2048)
_T_MERGE = (2048, 4096)
_T_ROWS = 4096


def _nt_dot(a, b):
    return lax.dot_general(a, b, (((1,), (1,)), ((), ())), preferred_element_type=F32)


def _tn_dot(a, b):
    return lax.dot_general(a, b, (((0,), (0,)), ((), ())), preferred_element_type=F32)


def _dot(a, b):
    return jnp.dot(a, b, preferred_element_type=F32)


def _const_spec(shape):
    nd = len(shape)
    return pl.BlockSpec(shape, lambda *_: (0,) * nd)


def _params(semantics):
    return pltpu.CompilerParams(dimension_semantics=semantics, vmem_limit_bytes=V7X_VMEM_LIMIT_BYTES)


def _rope_tables(seq, width):
    h2 = width // 4
    pos = jnp.arange(seq, dtype=jnp.int32)
    row = (pos // GRID_W).astype(F32)
    col = (pos % GRID_W).astype(F32)
    inv = ROPE_THETA ** (-jnp.arange(h2, dtype=F32) / h2)

    def part(p):
        ang = p[:, None] * inv[None, :]
        c, s = jnp.cos(ang), jnp.sin(ang)
        return jnp.concatenate([c, c], -1), jnp.concatenate([-s, s], -1)

    cr, sr = part(row)
    cc, sc = part(col)
    return jnp.concatenate([cr, cc], -1), jnp.concatenate([sr, sc], -1)


def _swap_lanes(x, width):
    h2 = width // 4
    lanes = x.shape[-1]
    lane = lax.broadcasted_iota(jnp.int32, x.shape, x.ndim - 1)
    return jnp.where((lane & h2) == 0, pltpu.roll(x, lanes - h2, x.ndim - 1), pltpu.roll(x, h2, x.ndim - 1))


def _swap_rows(x, width):
    h2 = width // 4
    p = [x[i * h2:(i + 1) * h2] for i in range(4)]
    return jnp.concatenate([p[1], p[0], p[3], p[2]], axis=0)


def _tables(seq, rotary):
    if rotary:
        c64, s64 = _rope_tables(seq, DIFF_HD)
        c32, s32 = _rope_tables(seq, MLA_ROPE)
    else:
        c64, s64 = jnp.ones((seq, DIFF_HD), F32), jnp.zeros((seq, DIFF_HD), F32)
        c32, s32 = jnp.ones((seq, MLA_ROPE), F32), jnp.zeros((seq, MLA_ROPE), F32)
    pad = jnp.zeros((seq, V7X_LANES - MLA_ROPE), F32)
    nat = jnp.concatenate([c64, c64, s64, s64, c32, pad, s32, pad], axis=-1)
    tr = jnp.concatenate([c32, s32, c64, s64], axis=-1).T
    return nat, tr


def _mod_kernel(c_ref, w_ref, b_ref, o_ref):
    c = c_ref[...]
    s = c * jax.nn.sigmoid(c)
    o_ref[...] = jnp.dot(s, w_ref[...], precision=lax.Precision.HIGHEST,
                         preferred_element_type=F32) + b_ref[...]


def _modulation(cc, w_mod, b_mod):
    rows, d = cc.shape
    n = w_mod.shape[1]
    return pl.pallas_call(
        _mod_kernel,
        grid=(n // d,),
        in_specs=[pl.BlockSpec((rows, d), lambda j: (0, 0)),
                  pl.BlockSpec((d, d), lambda j: (0, j)),
                  pl.BlockSpec((1, d), lambda j: (0, j))],
        out_specs=pl.BlockSpec((rows, d), lambda j: (0, j)),
        out_shape=jax.ShapeDtypeStruct((rows, n), F32),
        compiler_params=_params(("arbitrary",)),
        name="modulation",
    )(cc, w_mod, b_mod)


def _rms_rows(c, w):
    return c * lax.rsqrt(jnp.mean(c * c, axis=-1, keepdims=True) + RMS_EPS) * w


def _proj_kernel(x_ref, ctx_ref, scale_ref, shift_ref, scale_c_ref, shift_c_ref, wnat_ref, wt_ref, qn_ref,
                 kvn_ref, wuq_ref, wuk_ref, ekr_ref, wuv_ref, tabn_ref, tabt_ref,
                 ka_ref, vta_ref, kd_ref, vtd_ref, qta_ref, qtd_ref, gat_ref, gdt_ref, mt_ref, *, n_ctx_steps):
    n_kv, tk = vta_ref.shape[1], vta_ref.shape[-1]
    lanes = V7X_LANES
    step = pl.program_id(1)
    is_ctx = step < n_ctx_steps

    tok = jnp.where(is_ctx, ctx_ref[0], x_ref[0])
    scale = jnp.where(is_ctx, scale_c_ref[0], scale_ref[0])
    shift = jnp.where(is_ctx, shift_c_ref[0], shift_ref[0])
    h = (tok * (1.0 + scale) + shift).astype(BF16)

    mt_ref[0] = jax.nn.sigmoid(_nt_dot(wt_ref[_T_MERGE[0]:_T_MERGE[1], :], h)).astype(BF16)
    ga = _nt_dot(wt_ref[_T_GA[0]:_T_GA[1], :], h)
    gat_ref[0] = (ga * jax.nn.sigmoid(ga)).astype(BF16)
    gd = _nt_dot(wt_ref[_T_GD[0]:_T_GD[1], :], h)
    gdt_ref[0] = (gd * jax.nn.sigmoid(gd)).astype(BF16)

    pn = _dot(h, wnat_ref[...])

    ckv = _rms_rows(pn[:, _NAT_CKV[0]:_NAT_CKV[1]], kvn_ref[...]).astype(BF16)
    kr = pn[:, _NAT_KR[0]:_NAT_KR[1]]
    kr = kr * tabn_ref[:, 2 * lanes:3 * lanes] + _swap_lanes(kr, MLA_ROPE) * tabn_ref[:, 3 * lanes:4 * lanes]
    ka = _dot(ckv, wuk_ref[...]) + _dot(kr.astype(BF16), ekr_ref[...])
    ka_ref[0] = ka.astype(BF16)
    vta = _nt_dot(wuv_ref[...], ckv).astype(BF16)
    for j in range(n_kv):
        vta_ref[0, j] = vta[:, j * tk:(j + 1) * tk]

    for hh in range(DIFF_HEADS):
        a = _NAT_DK[0] + hh * lanes
        kd = pn[:, a:a + lanes]
        kd = kd * tabn_ref[:, 0:lanes] + _swap_lanes(kd, DIFF_HD) * tabn_ref[:, lanes:2 * lanes]
        kd_ref[0, :, hh * lanes:(hh + 1) * lanes] = kd.astype(BF16)

    vtd = _nt_dot(wt_ref[_T_DV[0]:_T_DV[1], :], h).astype(BF16)
    for j in range(n_kv):
        vtd_ref[0, j] = vtd[:, j * tk:(j + 1) * tk]


    cq = _rms_rows(pn[:, _NAT_CQ[0]:_NAT_CQ[1]], qn_ref[...]).astype(BF16)
    qta = _nt_dot(wuq_ref[...], cq) * (MLA_SCALE * LOG2E)
    c32, s32 = tabt_ref[0:MLA_ROPE, :], tabt_ref[MLA_ROPE:2 * MLA_ROPE, :]
    for hh in range(MLA_HEADS):
        base = hh * lanes
        r = qta[base + MLA_NOPE:base + MLA_NOPE + MLA_ROPE]
        r = r * c32 + _swap_rows(r, MLA_ROPE) * s32
        head = jnp.concatenate([qta[base:base + MLA_NOPE], r, qta[base + MLA_NOPE + MLA_ROPE:base + lanes]],
                               axis=0)
        qta_ref[0, hh, 0] = head.astype(BF16)

    qtd = _nt_dot(wt_ref[_T_DQ[0]:_T_DQ[1], :], h) * (DIFF_SCALE * LOG2E)
    c64 = tabt_ref[2 * MLA_ROPE:2 * MLA_ROPE + DIFF_HD, :]
    s64 = tabt_ref[2 * MLA_ROPE + DIFF_HD:2 * MLA_ROPE + 2 * DIFF_HD, :]
    for hh in range(DIFF_HEADS):
        halves = []
        for half in range(2):
            blk = qtd[(2 * hh + half) * DIFF_HD:(2 * hh + half + 1) * DIFF_HD]
            halves.append(blk * c64 + _swap_rows(blk, DIFF_HD) * s64)
        qtd_ref[0, hh, 0] = jnp.concatenate(halves, axis=0).astype(BF16)


def _projection(x, ctx, scale, shift, scale_c, shift_c, w, tabn, tabt):
    bsz, s, d = x.shape
    c = ctx.shape[1]
    ts, tk = PROJ_TOKENS, KV_TILE
    assert s % ts == 0 and c % ts == 0 and ts % tk == 0
    n_lat, n_ctx, n_kv = s // ts, c // ts, ts // tk
    n_total = s + c

    lat = lambda i: jnp.maximum(i - n_ctx, 0)
    kvb = lambda i: jnp.where(i < n_ctx, n_lat + i, i - n_ctx)
    consts = [w[name] for name in ("wnat", "wt", "qn", "kvn", "wuq", "wuk", "ekr", "wuv")]
    in_arrays = [x, ctx, scale, shift, scale_c, shift_c, *consts, tabn, tabt]
    in_specs = [pl.BlockSpec((1, ts, d), lambda b, i: (b, lat(i), 0)),
                pl.BlockSpec((1, ts, d), lambda b, i: (b, jnp.minimum(i, n_ctx - 1), 0)),
                pl.BlockSpec((1, 1, d), lambda b, i: (b, 0, 0)), pl.BlockSpec((1, 1, d), lambda b, i: (b, 0, 0)),
                pl.BlockSpec((1, 1, d), lambda b, i: (0, 0, 0)), pl.BlockSpec((1, 1, d), lambda b, i: (0, 0, 0)),
                *[_const_spec(a.shape) for a in consts],
                pl.BlockSpec((ts, tabn.shape[1]), lambda b, i: (kvb(i), 0)),
                pl.BlockSpec((tabt.shape[0], ts), lambda b, i: (0, lat(i)))]

    hp = MLA_HEADS * V7X_LANES
    feat = lambda rows: (jax.ShapeDtypeStruct((bsz, rows, s), BF16),
                         pl.BlockSpec((1, rows, ts), lambda b, i: (b, 0, lat(i))))
    def qtile(heads, q_tile):
        tq = min(q_tile, s)
        assert tq % ts == 0 and s % tq == 0
        per_q = tq // ts
        return (jax.ShapeDtypeStruct((bsz, heads, s // tq, V7X_LANES, tq), BF16),
                pl.BlockSpec((1, heads, 1, V7X_LANES, ts), lambda b, i: (b, 0, lat(i) // per_q, 0, lat(i) % per_q)))

    outs = [(jax.ShapeDtypeStruct((bsz, n_total, hp), BF16),
             pl.BlockSpec((1, ts, hp), lambda b, i: (b, kvb(i), 0))),
            (jax.ShapeDtypeStruct((bsz, n_total // tk, MLA_WIDTH, tk), BF16),
             pl.BlockSpec((1, n_kv, MLA_WIDTH, tk), lambda b, i: (b, kvb(i), 0, 0))),
            (jax.ShapeDtypeStruct((bsz, n_total, DIFF_WIDTH), BF16),
             pl.BlockSpec((1, ts, DIFF_WIDTH), lambda b, i: (b, kvb(i), 0))),
            (jax.ShapeDtypeStruct((bsz, n_total // tk, DIFF_WIDTH, tk), BF16),
             pl.BlockSpec((1, n_kv, DIFF_WIDTH, tk), lambda b, i: (b, kvb(i), 0, 0))),
            qtile(MLA_HEADS, MLA_Q_TILE), qtile(DIFF_HEADS, DIFF_Q_TILE),
            feat(MLA_WIDTH), feat(DIFF_WIDTH), feat(2 * D_MODEL)]
    return pl.pallas_call(
        functools.partial(_proj_kernel, n_ctx_steps=n_ctx),
        grid=(bsz, n_ctx + n_lat),
        in_specs=in_specs,
        out_specs=[o[1] for o in outs],
        out_shape=[o[0] for o in outs],
        compiler_params=_params(("parallel", "arbitrary")),
        name="projection",
    )(*in_arrays)


def _ones_rows(tk):
    rows = lax.broadcasted_iota(jnp.int32, (DENOM_ROWS, tk), 0)
    return jnp.where(rows == 0, 1.0, 0.0).astype(BF16)


def _softmax_stage(s, vts, ones, m_sc, acc_sc, first):
    m_new = jnp.max(s, axis=0, keepdims=True)
    if not first:
        m_old = m_sc[...]
        m_new = jnp.maximum(m_old, m_new)
        alpha = jnp.exp2(m_old - m_new)
    p = jnp.exp2(s - m_new).astype(BF16)
    vt_all = jnp.concatenate([jnp.concatenate([vt, ones], axis=0) for vt in vts], axis=1)
    pv = _dot(vt_all, p)
    acc_sc[...] = pv if first else alpha * acc_sc[...] + pv
    m_sc[...] = m_new


def _normalised(acc_sc, dv):
    acc = acc_sc[...]
    return acc[:dv] / acc[dv:dv + 1]


def _kv_group(n_tiles):
    return next(g for g in (3, 2, 1) if n_tiles % g == 0)


def _attention_pipeline(n_q, n_stage, scores, process, finish):
    scores(0, 0, 0)

    def body(qi, carry):
        nxt = jnp.minimum(qi + 1, n_q - 1)
        for j in range(n_stage):
            slot = j % SCORE_SLOTS
            last = j == n_stage - 1
            nq, nj = (nxt, 0) if last else (qi, j + 1)
            if nj % SCORE_SLOTS != slot:
                scores(nq, nj, nj % SCORE_SLOTS)
                process(j, slot)
            else:
                process(j, slot)
                scores(nq, nj, nj % SCORE_SLOTS)
        finish(qi)
        return carry

    lax.fori_loop(0, n_q, body, 0)


def _mla_attn_kernel(q_ref, k_ref, vt_ref, o_ref, s_sc, m_sc, acc_sc):
    n_q = q_ref.shape[2]
    n_tiles, dv, tk = vt_ref.shape[1:]
    g = _kv_group(n_tiles)
    rows = g * tk
    ones = _ones_rows(tk)

    def scores(qi, j, slot):
        s_sc[slot, 0] = _dot(k_ref[0, j * rows:(j + 1) * rows, :], q_ref[0, 0, qi])

    def process(j, slot):
        _softmax_stage(s_sc[slot, 0], [vt_ref[0, j * g + i] for i in range(g)], ones, m_sc, acc_sc, j == 0)

    def finish(qi):
        o_ref[0, 0, qi] = _normalised(acc_sc, dv).astype(o_ref.dtype)

    _attention_pipeline(n_q, n_tiles // g, scores, process, finish)


def _diff_attn_kernel(q_ref, k_ref, vt_ref, lam_a_ref, lam_b_ref, subln_ref, o_ref, s_sc, m1, a1, m2, a2):
    n_q = q_ref.shape[2]
    n_tiles, dv, tk = vt_ref.shape[1:]
    g = _kv_group(n_tiles)
    rows = g * tk
    ones = _ones_rows(tk)
    zeros = jnp.zeros((DIFF_HD, q_ref.shape[-1]), q_ref.dtype)

    e = jnp.exp(jnp.sum(lam_a_ref[...] * lam_b_ref[...], axis=1, keepdims=True))
    sign = jnp.where(lax.broadcasted_iota(jnp.int32, e.shape, 0) == 0, 1.0, -1.0)
    lam = jnp.sum(e * sign, axis=0, keepdims=True) + LAM_INIT

    def scores(qi, j, slot):
        q = q_ref[0, 0, qi]
        k = k_ref[0, j * rows:(j + 1) * rows, :]
        s_sc[slot, 0] = _dot(k, jnp.concatenate([q[:DIFF_HD], zeros], axis=0))
        s_sc[slot, 1] = _dot(k, jnp.concatenate([zeros, q[DIFF_HD:]], axis=0))

    def process(j, slot):
        vts = [vt_ref[0, j * g + i] for i in range(g)]
        _softmax_stage(s_sc[slot, 0], vts, ones, m1, a1, j == 0)
        _softmax_stage(s_sc[slot, 1], vts, ones, m2, a2, j == 0)

    def finish(qi):
        o = _normalised(a1, dv) - lam * _normalised(a2, dv)
        o = o * lax.rsqrt(jnp.mean(o * o, axis=0, keepdims=True) + SUBLN_EPS)
        o_ref[0, 0, qi] = ((o * subln_ref[...]) * (1.0 - LAM_INIT)).astype(o_ref.dtype)

    _attention_pipeline(n_q, n_tiles // g, scores, process, finish)


def _attention(qt, k, vt, *, dv, extra=(), kernel_fn, n_softmax, name):
    bsz, heads, n_q, lanes, tq = qt.shape
    t = k.shape[1]
    n_tiles, tk = vt.shape[1], vt.shape[-1]
    assert n_tiles * tk == t and lanes == V7X_LANES
    in_specs = [pl.BlockSpec((1, 1, n_q, lanes, tq), lambda b, h: (b, h, 0, 0, 0)),
                pl.BlockSpec((1, t, lanes), lambda b, h: (b, 0, h)),
                pl.BlockSpec((1, n_tiles, dv, tk), lambda b, h: (b, 0, h, 0))]
    in_specs += [_const_spec(a.shape) for a in extra]
    rows = _kv_group(n_tiles) * tk
    scratch = [pltpu.VMEM((SCORE_SLOTS, n_softmax, rows, tq), F32)]
    for _ in range(n_softmax):
        scratch += [pltpu.VMEM((1, tq), F32), pltpu.VMEM((dv + DENOM_ROWS, tq), F32)]
    return pl.pallas_call(
        kernel_fn,
        grid=(bsz, heads),
        in_specs=in_specs,
        out_specs=pl.BlockSpec((1, 1, n_q, dv, tq), lambda b, h: (b, h, 0, 0, 0)),
        out_shape=jax.ShapeDtypeStruct((bsz, heads, n_q, dv, tq), BF16),
        scratch_shapes=scratch,
        compiler_params=_params(("parallel", "parallel")),
        name=name,
    )(qt, k, vt, *extra)


def _out_kernel(oa_ref, od_ref, ga_ref, gd_ref, m_ref, x_ref, gate_ref, woa_ref, wob_ref, wout_ref,
                g_ref, b_ref, o_ref):
    tf = x_ref.shape[1]
    chunk = min(OUT_CHUNK, tf)

    def heads_major(ref, c0):
        tq = ref.shape[-1]
        t, off = c0 // tq, c0 % tq
        return jnp.concatenate([ref[0, hh, t, :, off:off + chunk] for hh in range(ref.shape[1])], axis=0)

    for c0 in range(0, tf, chunk):
        sl = slice(c0, c0 + chunk)
        ua = (heads_major(oa_ref, c0).astype(F32) * ga_ref[0, :, sl].astype(F32)).astype(BF16)
        ud = (heads_major(od_ref, c0).astype(F32) * gd_ref[0, :, sl].astype(F32)).astype(BF16)
        ya = _dot(woa_ref[...], ua)
        yd = _dot(wob_ref[...], ud)
        mix = m_ref[0, :D_MODEL, sl].astype(F32) * ya + m_ref[0, D_MODEL:, sl].astype(F32) * yd
        y = _tn_dot(mix.astype(BF16), wout_ref[...])
        z = ALPHA * x_ref[0, sl, :] + gate_ref[0] * y
        mu = jnp.mean(z, axis=-1, keepdims=True)
        zc = z - mu
        var = jnp.mean(zc * zc, axis=-1, keepdims=True)
        o_ref[0, sl, :] = zc * lax.rsqrt(var + LN_EPS) * g_ref[...] + b_ref[...]


def _output(oa, od, ga, gd, mt, x, gate, w):
    bsz, s, d = x.shape
    tf = max(oa.shape[-1], od.shape[-1])
    feat = lambda rows: pl.BlockSpec((1, rows, tf), lambda b, i: (b, 0, i))
    qtile = lambda a: pl.BlockSpec((1, a.shape[1], tf // a.shape[-1], a.shape[3], a.shape[-1]),
                                   lambda b, i: (b, 0, i, 0, 0))
    return pl.pallas_call(
        _out_kernel,
        grid=(bsz, s // tf),
        in_specs=[qtile(oa), qtile(od), feat(MLA_WIDTH), feat(DIFF_WIDTH), feat(2 * d),
                  pl.BlockSpec((1, tf, d), lambda b, i: (b, i, 0)),
                  pl.BlockSpec((1, 1, d), lambda b, i: (b, 0, 0)),
                  _const_spec(w["woa"].shape), _const_spec(w["wob"].shape), _const_spec(w["wout"].shape),
                  _const_spec(w["ln_g"].shape), _const_spec(w["ln_b"].shape)],
        out_specs=pl.BlockSpec((1, tf, d), lambda b, i: (b, i, 0)),
        out_shape=jax.ShapeDtypeStruct((bsz, s, d), F32),
        compiler_params=_params(("parallel", "parallel")),
        name="output",
    )(oa, od, ga, gd, mt, x, gate, w["woa"], w["wob"], w["wout"], w["ln_g"], w["ln_b"])


def _prep_weights(w_in, mla_q_norm, mla_kv_norm, w_uq, w_ukv, w_oa, w_ob, w_out, ln_g, ln_b):
    o = IN_OFFSETS
    w_in, w_uq, w_ukv = w_in.astype(BF16), w_uq.astype(BF16), w_ukv.astype(BF16)
    cols = lambda i: w_in[:, o[i]:o[i + 1]]
    w_cq, w_ckv, w_kr, w_ga, w_dq, w_dk, w_dv, w_gd, w_mg = (cols(i) for i in range(9))
    lanes = V7X_LANES
    pad_kr = jnp.zeros((D_MODEL, lanes - MLA_ROPE), BF16)
    wnat = jnp.concatenate([w_cq, w_ckv, w_dk, w_kr, pad_kr], axis=1)
    assert wnat.shape[1] == _NAT_COLS
    wt = jnp.concatenate([w_dq, w_dv, w_ga, w_gd, w_mg], axis=1).T
    assert wt.shape[0] == _T_ROWS

    uq = w_uq.reshape(MLA_Q_LORA, MLA_HEADS, MLA_NOPE + MLA_ROPE)
    uq = jnp.pad(uq, ((0, 0), (0, 0), (0, lanes - MLA_NOPE - MLA_ROPE)))
    wuq = uq.reshape(MLA_Q_LORA, MLA_HEADS * lanes).T
    ukv = w_ukv.reshape(MLA_KV_LORA, MLA_HEADS, MLA_NOPE + MLA_V)
    wuk = jnp.pad(ukv[:, :, :MLA_NOPE], ((0, 0), (0, 0), (0, lanes - MLA_NOPE)))
    wuk = wuk.reshape(MLA_KV_LORA, MLA_HEADS * lanes)
    wuv = ukv[:, :, MLA_NOPE:].reshape(MLA_KV_LORA, MLA_WIDTH).T
    ekr = np.zeros((lanes, MLA_HEADS * lanes), np.float32)
    for hh in range(MLA_HEADS):
        ekr[np.arange(MLA_ROPE), hh * lanes + MLA_NOPE + np.arange(MLA_ROPE)] = 1.0
    return {
        "wnat": wnat, "wt": wt,
        "qn": mla_q_norm[None, :], "kvn": mla_kv_norm[None, :],
        "wuq": wuq, "wuk": wuk, "wuv": wuv,
        "ekr": jnp.asarray(ekr, BF16),
        "woa": w_oa.T.astype(BF16), "wob": w_ob.T.astype(BF16), "wout": w_out.astype(BF16),
        "ln_g": ln_g[None, :], "ln_b": ln_b[None, :],
    }


def kernel(x, c, ctx, c_ctx, w_mod, b_mod, w_in, mla_q_norm, mla_kv_norm, w_uq, w_ukv, diff_lambda,
           diff_subln, w_oa, w_ob, w_out, ln_g, ln_b):
    bsz, s, d = x.shape
    n_ctx = ctx.shape[1]
    assert w_mod.shape[0] == DEPTH == 1 and d == D_MODEL and s % GRID_W == 0 and bsz + 1 <= MOD_ROWS

    cc = jnp.concatenate([c, c_ctx[None, :], jnp.zeros((MOD_ROWS - bsz - 1, d), F32)], axis=0)
    mod = _modulation(cc, w_mod[0], b_mod[0][None, :])
    shift, scale, gate = (mod[:bsz, i * d:(i + 1) * d][:, None, :] for i in range(3))
    shift_c, scale_c = (mod[bsz:bsz + 1, i * d:(i + 1) * d][:, None, :] for i in range(2))

    w = _prep_weights(w_in[0], mla_q_norm[0], mla_kv_norm[0], w_uq[0], w_ukv[0], w_oa[0], w_ob[0], w_out[0],
                      ln_g[0], ln_b[0])
    tabn, tabt = _tables(s, rotary=True)
    tabn_c, _ = _tables(n_ctx, rotary=False)
    tabn = jnp.concatenate([tabn, tabn_c], axis=0)

    ka, vta, kd, vtd, qta, qtd, gat, gdt, mt = _projection(x, ctx, scale, shift, scale_c, shift_c, w, tabn, tabt)

    oa = _attention(qta, ka, vta, dv=MLA_V, kernel_fn=_mla_attn_kernel, n_softmax=1, name="attention_mla")
    lam_a = diff_lambda[0][0::2]
    lam_b = diff_lambda[0][1::2]
    od = _attention(qtd, kd, vtd, dv=2 * DIFF_HD, extra=(lam_a, lam_b, diff_subln[0][:, None]),
                    kernel_fn=_diff_attn_kernel, n_softmax=2, name="attention_diff")
    return _output(oa, od, gat, gdt, mt, x, gate, w)
```

```python
import functools
import math

import numpy as np
import jax
import jax.numpy as jnp
from jax import lax
from jax.experimental import pallas as pl
from jax.experimental.pallas import tpu as pltpu

F32 = jnp.float32
BF16 = jnp.bfloat16

D_MODEL = 1024
GRID_W = 64
ROPE_THETA = 10000.0
LN_EPS = 1e-5
RMS_EPS = 1e-6
SUBLN_EPS = 1e-5
MLA_HEADS = 8
MLA_NOPE = 64
MLA_ROPE = 32
MLA_V = 64
MLA_Q_LORA = 384
MLA_KV_LORA = 256
MLA_WIDTH = MLA_HEADS * MLA_V
MLA_SCALE = (MLA_NOPE + MLA_ROPE) ** -0.5
DIFF_HEADS = 4
DIFF_HD = 64
DIFF_WIDTH = DIFF_HEADS * 2 * DIFF_HD
DIFF_SCALE = DIFF_HD ** -0.5
DEPTH = 1
ALPHA = (2 * DEPTH) ** 0.25
LAM_INIT = 0.8 - 0.6 * math.exp(-0.3 * 0)
IN_SPLITS = (MLA_Q_LORA, MLA_KV_LORA, MLA_ROPE, MLA_WIDTH, DIFF_WIDTH, DIFF_WIDTH, DIFF_WIDTH, DIFF_WIDTH,
             2 * D_MODEL)
IN_OFFSETS = tuple(int(o) for o in np.cumsum((0,) + IN_SPLITS))
LOG2E = math.log2(math.e)

V7X_LANES = 128
V7X_VMEM_LIMIT_BYTES = 56 * 2**20
MOD_ROWS = 16
DENOM_ROWS = 16

PROJ_TOKENS = 256
KV_TILE = 256
MLA_Q_TILE = 512
DIFF_Q_TILE = 256
SCORE_SLOTS = 2
OUT_CHUNK = 256

_NAT_CQ = (0, 384)
_NAT_CKV = (384, 640)
_NAT_DK = (640, 1152)
_NAT_KR = (1152, 1280)
_NAT_COLS = 1280
_T_DQ = (0, 512)
_T_DV = (512, 1024)
_T_GA = (1024, 1536)
_T_GD = (1536, 2048)
_T_MERGE = (2048, 4096)
_T_ROWS = 4096


def _nt_dot(a, b):
    return lax.dot_general(a, b, (((1,), (1,)), ((), ())), preferred_element_type=F32)


def _tn_dot(a, b):
    return lax.dot_general(a, b, (((0,), (0,)), ((), ())), preferred_element_type=F32)


def _dot(a, b):
    return jnp.dot(a, b, preferred_element_type=F32)


def _const_spec(shape):
    nd = len(shape)
    return pl.BlockSpec(shape, lambda *_: (0,) * nd)


def _params(semantics):
    return pltpu.CompilerParams(dimension_semantics=semantics, vmem_limit_bytes=V7X_VMEM_LIMIT_BYTES)


def _rope_tables(seq, width):
    h2 = width // 4
    pos = jnp.arange(seq, dtype=jnp.int32)
    row = (pos // GRID_W).astype(F32)
    col = (pos % GRID_W).astype(F32)
    inv = ROPE_THETA ** (-jnp.arange(h2, dtype=F32) / h2)

    def part(p):
        ang = p[:, None] * inv[None, :]
        c, s = lax.optimization_barrier((jnp.cos(ang), jnp.sin(ang)))
        return jnp.concatenate([c, c], -1), jnp.concatenate([-s, s], -1)

    cr, sr = part(row)
    cc, sc = part(col)
    return jnp.concatenate([cr, cc], -1), jnp.concatenate([sr, sc], -1)


def _swap_lanes(x, width):
    h2 = width // 4
    lanes = x.shape[-1]
    lane = lax.broadcasted_iota(jnp.int32, x.shape, x.ndim - 1)
    return jnp.where((lane & h2) == 0, pltpu.roll(x, lanes - h2, x.ndim - 1), pltpu.roll(x, h2, x.ndim - 1))


def _swap_rows(x, width):
    h2 = width // 4
    p = [x[i * h2:(i + 1) * h2] for i in range(4)]
    return jnp.concatenate([p[1], p[0], p[3], p[2]], axis=0)


def _tables(seq, rotary):
    if rotary:
        c64, s64 = _rope_tables(seq, DIFF_HD)
        c32, s32 = _rope_tables(seq, MLA_ROPE)
    else:
        c64, s64 = jnp.ones((seq, DIFF_HD), F32), jnp.zeros((seq, DIFF_HD), F32)
        c32, s32 = jnp.ones((seq, MLA_ROPE), F32), jnp.zeros((seq, MLA_ROPE), F32)
    pad = jnp.zeros((seq, V7X_LANES - MLA_ROPE), F32)
    nat = jnp.concatenate([c64, c64, s64, s64, c32, pad, s32, pad], axis=-1)
    tr = jnp.concatenate([c32, s32, c64, s64], axis=-1).T
    return nat, tr


def _mod_kernel(c_ref, w_ref, b_ref, o_ref):
    c = c_ref[...]
    s = c * jax.nn.sigmoid(c)
    o_ref[...] = jnp.dot(s, w_ref[...], precision=lax.Precision.HIGHEST,
                         preferred_element_type=F32) + b_ref[...]


def _modulation(cc, w_mod, b_mod):
    rows, d = cc.shape
    n = w_mod.shape[1]
    return pl.pallas_call(
        _mod_kernel,
        grid=(n // d,),
        in_specs=[pl.BlockSpec((rows, d), lambda j: (0, 0)),
                  pl.BlockSpec((d, d), lambda j: (0, j)),
                  pl.BlockSpec((1, d), lambda j: (0, j))],
        out_specs=pl.BlockSpec((rows, d), lambda j: (0, j)),
        out_shape=jax.ShapeDtypeStruct((rows, n), F32),
        compiler_params=_params(("arbitrary",)),
        name="modulation",
    )(cc, w_mod, b_mod)


def _rms_rows(c, w):
    return c * lax.rsqrt(jnp.mean(c * c, axis=-1, keepdims=True) + RMS_EPS) * w


def _proj_kernel(x_ref, ctx_ref, scale_ref, shift_ref, scale_c_ref, shift_c_ref, wnat_ref, wt_ref, qn_ref,
                 kvn_ref, wuq_ref, wuk_ref, ekr_ref, wuv_ref, tabn_ref, tabt_ref,
                 ka_ref, vta_ref, kd_ref, vtd_ref, qta_ref, qtd_ref, gat_ref, gdt_ref, mt_ref, *, n_ctx_steps):
    n_kv, tk = vta_ref.shape[1], vta_ref.shape[-1]
    lanes = V7X_LANES
    step = pl.program_id(1)
    is_ctx = step < n_ctx_steps

    tok = jnp.where(is_ctx, ctx_ref[0], x_ref[0])
    scale = jnp.where(is_ctx, scale_c_ref[0], scale_ref[0])
    shift = jnp.where(is_ctx, shift_c_ref[0], shift_ref[0])
    h = (tok * (1.0 + scale) + shift).astype(BF16)

    mt_ref[0] = jax.nn.sigmoid(_nt_dot(wt_ref[_T_MERGE[0]:_T_MERGE[1], :], h)).astype(BF16)
    ga = _nt_dot(wt_ref[_T_GA[0]:_T_GA[1], :], h)
    gat_ref[0] = (ga * jax.nn.sigmoid(ga)).astype(BF16)
    gd = _nt_dot(wt_ref[_T_GD[0]:_T_GD[1], :], h)
    gdt_ref[0] = (gd * jax.nn.sigmoid(gd)).astype(BF16)

    pn = _dot(h, wnat_ref[...])

    ckv = _rms_rows(pn[:, _NAT_CKV[0]:_NAT_CKV[1]], kvn_ref[...]).astype(BF16)
    kr = pn[:, _NAT_KR[0]:_NAT_KR[1]]
    kr = kr * tabn_ref[:, 2 * lanes:3 * lanes] + _swap_lanes(kr, MLA_ROPE) * tabn_ref[:, 3 * lanes:4 * lanes]
    ka = _dot(ckv, wuk_ref[...]) + _dot(kr.astype(BF16), ekr_ref[...])
    ka_ref[0] = ka.astype(BF16)
    vta = _nt_dot(wuv_ref[...], ckv).astype(BF16)
    for j in range(n_kv):
        vta_ref[0, j] = vta[:, j * tk:(j + 1) * tk]

    for hh in range(DIFF_HEADS):
        a = _NAT_DK[0] + hh * lanes
        kd = pn[:, a:a + lanes]
        kd = kd * tabn_ref[:, 0:lanes] + _swap_lanes(kd, DIFF_HD) * tabn_ref[:, lanes:2 * lanes]
        kd_ref[0, :, hh * lanes:(hh + 1) * lanes] = kd.astype(BF16)

    vtd = _nt_dot(wt_ref[_T_DV[0]:_T_DV[1], :], h).astype(BF16)
    for j in range(n_kv):
        vtd_ref[0, j] = vtd[:, j * tk:(j + 1) * tk]


    cq = _rms_rows(pn[:, _NAT_CQ[0]:_NAT_CQ[1]], qn_ref[...]).astype(BF16)
    qta = _nt_dot(wuq_ref[...], cq) * (MLA_SCALE * LOG2E)
    c32, s32 = tabt_ref[0:MLA_ROPE, :], tabt_ref[MLA_ROPE:2 * MLA_ROPE, :]
    for hh in range(MLA_HEADS):
        base = hh * lanes
        r = qta[base + MLA_NOPE:base + MLA_NOPE + MLA_ROPE]
        r = r * c32 + _swap_rows(r, MLA_ROPE) * s32
        head = jnp.concatenate([qta[base:base + MLA_NOPE], r, qta[base + MLA_NOPE + MLA_ROPE:base + lanes]],
                               axis=0)
        qta_ref[0, hh, 0] = head.astype(BF16)

    qtd = _nt_dot(wt_ref[_T_DQ[0]:_T_DQ[1], :], h) * (DIFF_SCALE * LOG2E)
    c64 = tabt_ref[2 * MLA_ROPE:2 * MLA_ROPE + DIFF_HD, :]
    s64 = tabt_ref[2 * MLA_ROPE + DIFF_HD:2 * MLA_ROPE + 2 * DIFF_HD, :]
    for hh in range(DIFF_HEADS):
        halves = []
        for half in range(2):
            blk = qtd[(2 * hh + half) * DIFF_HD:(2 * hh + half + 1) * DIFF_HD]
            halves.append(blk * c64 + _swap_rows(blk, DIFF_HD) * s64)
        qtd_ref[0, hh, 0] = jnp.concatenate(halves, axis=0).astype(BF16)


def _projection(x, ctx, scale, shift, scale_c, shift_c, w, tabn, tabt):
    bsz, s, d = x.shape
    c = ctx.shape[1]
    ts, tk = PROJ_TOKENS, KV_TILE
    assert s % ts == 0 and c % ts == 0 and ts % tk == 0
    n_lat, n_ctx, n_kv = s // ts, c // ts, ts // tk
    n_total = s + c

    lat = lambda i: jnp.maximum(i - n_ctx, 0)
    kvb = lambda i: jnp.where(i < n_ctx, n_lat + i, i - n_ctx)
    consts = [w[name] for name in ("wnat", "wt", "qn", "kvn", "wuq", "wuk", "ekr", "wuv")]
    in_arrays = [x, ctx, scale, shift, scale_c, shift_c, *consts, tabn, tabt]
    in_specs = [pl.BlockSpec((1, ts, d), lambda b, i: (b, lat(i), 0)),
                pl.BlockSpec((1, ts, d), lambda b, i: (b, jnp.minimum(i, n_ctx - 1), 0)),
                pl.BlockSpec((1, 1, d), lambda b, i: (b, 0, 0)), pl.BlockSpec((1, 1, d), lambda b, i: (b, 0, 0)),
                pl.BlockSpec((1, 1, d), lambda b, i: (0, 0, 0)), pl.BlockSpec((1, 1, d), lambda b, i: (0, 0, 0)),
                *[_const_spec(a.shape) for a in consts],
                pl.BlockSpec((ts, tabn.shape[1]), lambda b, i: (kvb(i), 0)),
                pl.BlockSpec((tabt.shape[0], ts), lambda b, i: (0, lat(i)))]

    hp = MLA_HEADS * V7X_LANES
    feat = lambda rows: (jax.ShapeDtypeStruct((bsz, rows, s), BF16),
                         pl.BlockSpec((1, rows, ts), lambda b, i: (b, 0, lat(i))))
    def qtile(heads, q_tile):
        tq = min(q_tile, s)
        assert tq % ts == 0 and s % tq == 0
        per_q = tq // ts
        return (jax.ShapeDtypeStruct((bsz, heads, s // tq, V7X_LANES, tq), BF16),
                pl.BlockSpec((1, heads, 1, V7X_LANES, ts), lambda b, i: (b, 0, lat(i) // per_q, 0, lat(i) % per_q)))

    outs = [(jax.ShapeDtypeStruct((bsz, n_total, hp), BF16),
             pl.BlockSpec((1, ts, hp), lambda b, i: (b, kvb(i), 0))),
            (jax.ShapeDtypeStruct((bsz, n_total // tk, MLA_WIDTH, tk), BF16),
             pl.BlockSpec((1, n_kv, MLA_WIDTH, tk), lambda b, i: (b, kvb(i), 0, 0))),
            (jax.ShapeDtypeStruct((bsz, n_total, DIFF_WIDTH), BF16),
             pl.BlockSpec((1, ts, DIFF_WIDTH), lambda b, i: (b, kvb(i), 0))),
            (jax.ShapeDtypeStruct((bsz, n_total // tk, DIFF_WIDTH, tk), BF16),
             pl.BlockSpec((1, n_kv, DIFF_WIDTH, tk), lambda b, i: (b, kvb(i), 0, 0))),
            qtile(MLA_HEADS, MLA_Q_TILE), qtile(DIFF_HEADS, DIFF_Q_TILE),
            feat(MLA_WIDTH), feat(DIFF_WIDTH), feat(2 * D_MODEL)]
    return pl.pallas_call(
        functools.partial(_proj_kernel, n_ctx_steps=n_ctx),
        grid=(bsz, n_ctx + n_lat),
        in_specs=in_specs,
        out_specs=[o[1] for o in outs],
        out_shape=[o[0] for o in outs],
        compiler_params=_params(("parallel", "arbitrary")),
        name="projection",
    )(*in_arrays)


def _ones_rows(tk):
    rows = lax.broadcasted_iota(jnp.int32, (DENOM_ROWS, tk), 0)
    return jnp.where(rows == 0, 1.0, 0.0).astype(BF16)


def _softmax_stage(s, vts, ones, stats):
    m_new = jnp.max(s, axis=0, keepdims=True)
    if stats is not None:
        m_old, acc_old = stats
        m_new = jnp.maximum(m_old, m_new)
        alpha = jnp.exp2(m_old - m_new)
    p = jnp.exp2(s - m_new).astype(BF16)
    vt_all = jnp.concatenate([jnp.concatenate([vt, ones], axis=0) for vt in vts], axis=1)
    pv = _dot(vt_all, p)
    return m_new, (pv if stats is None else alpha * acc_old + pv)


def _normalised(stats, dv):
    acc = stats[1]
    return acc[:dv] / acc[dv:dv + 1]


def _kv_group(n_tiles):
    return next(g for g in (3, 2, 1) if n_tiles % g == 0)


def _attention_pipeline(n_q, n_stage, scores, process, finish):
    scores(0, 0, 0)

    def body(qi, carry):
        nxt = jnp.minimum(qi + 1, n_q - 1)
        stats = None
        for j in range(n_stage):
            slot = j % SCORE_SLOTS
            last = j == n_stage - 1
            nq, nj = (nxt, 0) if last else (qi, j + 1)
            if nj % SCORE_SLOTS != slot:
                scores(nq, nj, nj % SCORE_SLOTS)
                stats = process(j, slot, stats)
            else:
                stats = process(j, slot, stats)
                scores(nq, nj, nj % SCORE_SLOTS)
        finish(qi, stats)
        return carry

    lax.fori_loop(0, n_q, body, 0)


def _mla_attn_kernel(q_ref, k_ref, vt_ref, o_ref, s_sc):
    n_q = q_ref.shape[2]
    n_tiles, dv, tk = vt_ref.shape[1:]
    g = _kv_group(n_tiles)
    rows = g * tk
    ones = _ones_rows(tk)

    def scores(qi, j, slot):
        s_sc[slot, 0] = _dot(k_ref[0, j * rows:(j + 1) * rows, :], q_ref[0, 0, qi])

    def process(j, slot, stats):
        return _softmax_stage(s_sc[slot, 0], [vt_ref[0, j * g + i] for i in range(g)], ones, stats)

    def finish(qi, stats):
        o_ref[0, 0, qi] = _normalised(stats, dv).astype(o_ref.dtype)

    _attention_pipeline(n_q, n_tiles // g, scores, process, finish)


def _diff_attn_kernel(q_ref, k_ref, vt_ref, lam_a_ref, lam_b_ref, subln_ref, o_ref, s_sc):
    n_q = q_ref.shape[2]
    n_tiles, dv, tk = vt_ref.shape[1:]
    g = _kv_group(n_tiles)
    rows = g * tk
    ones = _ones_rows(tk)
    zeros = jnp.zeros((DIFF_HD, q_ref.shape[-1]), q_ref.dtype)

    e = jnp.exp(jnp.sum(lam_a_ref[...] * lam_b_ref[...], axis=1, keepdims=True))
    sign = jnp.where(lax.broadcasted_iota(jnp.int32, e.shape, 0) == 0, 1.0, -1.0)
    lam = jnp.sum(e * sign, axis=0, keepdims=True) + LAM_INIT

    def scores(qi, j, slot):
        q = q_ref[0, 0, qi]
        k = k_ref[0, j * rows:(j + 1) * rows, :]
        s_sc[slot, 0] = _dot(k, jnp.concatenate([q[:DIFF_HD], zeros], axis=0))
        s_sc[slot, 1] = _dot(k, jnp.concatenate([zeros, q[DIFF_HD:]], axis=0))

    def process(j, slot, stats):
        vts = [vt_ref[0, j * g + i] for i in range(g)]
        st1, st2 = stats if stats is not None else (None, None)
        return _softmax_stage(s_sc[slot, 0], vts, ones, st1), _softmax_stage(s_sc[slot, 1], vts, ones, st2)

    def finish(qi, stats):
        o = _normalised(stats[0], dv) - lam * _normalised(stats[1], dv)
        o = o * lax.rsqrt(jnp.mean(o * o, axis=0, keepdims=True) + SUBLN_EPS)
        o_ref[0, 0, qi] = ((o * subln_ref[...]) * (1.0 - LAM_INIT)).astype(o_ref.dtype)

    _attention_pipeline(n_q, n_tiles // g, scores, process, finish)


def _attention(qt, k, vt, *, dv, extra=(), kernel_fn, n_softmax, name):
    bsz, heads, n_q, lanes, tq = qt.shape
    t = k.shape[1]
    n_tiles, tk = vt.shape[1], vt.shape[-1]
    assert n_tiles * tk == t and lanes == V7X_LANES
    in_specs = [pl.BlockSpec((1, 1, n_q, lanes, tq), lambda b, h: (b, h, 0, 0, 0)),
                pl.BlockSpec((1, t, lanes), lambda b, h: (b, 0, h)),
                pl.BlockSpec((1, n_tiles, dv, tk), lambda b, h: (b, 0, h, 0))]
    in_specs += [_const_spec(a.shape) for a in extra]
    rows = _kv_group(n_tiles) * tk
    scratch = [pltpu.VMEM((SCORE_SLOTS, n_softmax, rows, tq), F32)]
    return pl.pallas_call(
        kernel_fn,
        grid=(bsz, heads),
        in_specs=in_specs,
        out_specs=pl.BlockSpec((1, 1, n_q, dv, tq), lambda b, h: (b, h, 0, 0, 0)),
        out_shape=jax.ShapeDtypeStruct((bsz, heads, n_q, dv, tq), BF16),
        scratch_shapes=scratch,
        compiler_params=_params(("parallel", "parallel")),
        name=name,
    )(qt, k, vt, *extra)


def _out_kernel(oa_ref, od_ref, ga_ref, gd_ref, m_ref, x_ref, gate_ref, woa_ref, wob_ref, wout_ref,
                g_ref, b_ref, o_ref):
    tf = x_ref.shape[1]
    chunk = min(OUT_CHUNK, tf)

    def heads_major(ref, c0):
        tq = ref.shape[-1]
        t, off = c0 // tq, c0 % tq
        return jnp.concatenate([ref[0, hh, t, :, off:off + chunk] for hh in range(ref.shape[1])], axis=0)

    for c0 in range(0, tf, chunk):
        sl = slice(c0, c0 + chunk)
        ua = (heads_major(oa_ref, c0).astype(F32) * ga_ref[0, :, sl].astype(F32)).astype(BF16)
        ud = (heads_major(od_ref, c0).astype(F32) * gd_ref[0, :, sl].astype(F32)).astype(BF16)
        ya = _dot(woa_ref[...], ua)
        yd = _dot(wob_ref[...], ud)
        mix = m_ref[0, :D_MODEL, sl].astype(F32) * ya + m_ref[0, D_MODEL:, sl].astype(F32) * yd
        y = _tn_dot(mix.astype(BF16), wout_ref[...])
        z = ALPHA * x_ref[0, sl, :] + gate_ref[0] * y
        mu = jnp.mean(z, axis=-1, keepdims=True)
        zc = z - mu
        var = jnp.mean(zc * zc, axis=-1, keepdims=True)
        o_ref[0, sl, :] = zc * lax.rsqrt(var + LN_EPS) * g_ref[...] + b_ref[...]


def _output(oa, od, ga, gd, mt, x, gate, w):
    bsz, s, d = x.shape
    tf = max(oa.shape[-1], od.shape[-1])
    feat = lambda rows: pl.BlockSpec((1, rows, tf), lambda b, i: (b, 0, i))
    qtile = lambda a: pl.BlockSpec((1, a.shape[1], tf // a.shape[-1], a.shape[3], a.shape[-1]),
                                   lambda b, i: (b, 0, i, 0, 0))
    return pl.pallas_call(
        _out_kernel,
        grid=(bsz, s // tf),
        in_specs=[qtile(oa), qtile(od), feat(MLA_WIDTH), feat(DIFF_WIDTH), feat(2 * d),
                  pl.BlockSpec((1, tf, d), lambda b, i: (b, i, 0)),
                  pl.BlockSpec((1, 1, d), lambda b, i: (b, 0, 0)),
                  _const_spec(w["woa"].shape), _const_spec(w["wob"].shape), _const_spec(w["wout"].shape),
                  _const_spec(w["ln_g"].shape), _const_spec(w["ln_b"].shape)],
        out_specs=pl.BlockSpec((1, tf, d), lambda b, i: (b, i, 0)),
        out_shape=jax.ShapeDtypeStruct((bsz, s, d), F32),
        compiler_params=_params(("parallel", "parallel")),
        name="output",
    )(oa, od, ga, gd, mt, x, gate, w["woa"], w["wob"], w["wout"], w["ln_g"], w["ln_b"])


def _prep_weights(w_in, mla_q_norm, mla_kv_norm, w_uq, w_ukv, w_oa, w_ob, w_out, ln_g, ln_b):
    o = IN_OFFSETS
    w_in, w_uq, w_ukv = w_in.astype(BF16), w_uq.astype(BF16), w_ukv.astype(BF16)
    cols = lambda i: w_in[:, o[i]:o[i + 1]]
    w_cq, w_ckv, w_kr, w_ga, w_dq, w_dk, w_dv, w_gd, w_mg = (cols(i) for i in range(9))
    lanes = V7X_LANES
    pad_kr = jnp.zeros((D_MODEL, lanes - MLA_ROPE), BF16)
    wnat = jnp.concatenate([w_cq, w_ckv, w_dk, w_kr, pad_kr], axis=1)
    assert wnat.shape[1] == _NAT_COLS
    wt = jnp.concatenate([w_dq, w_dv, w_ga, w_gd, w_mg], axis=1).T
    assert wt.shape[0] == _T_ROWS

    uq = w_uq.reshape(MLA_Q_LORA, MLA_HEADS, MLA_NOPE + MLA_ROPE)
    uq = jnp.pad(uq, ((0, 0), (0, 0), (0, lanes - MLA_NOPE - MLA_ROPE)))
    wuq = uq.reshape(MLA_Q_LORA, MLA_HEADS * lanes).T
    ukv = w_ukv.reshape(MLA_KV_LORA, MLA_HEADS, MLA_NOPE + MLA_V)
    wuk = jnp.pad(ukv[:, :, :MLA_NOPE], ((0, 0), (0, 0), (0, lanes - MLA_NOPE)))
    wuk = wuk.reshape(MLA_KV_LORA, MLA_HEADS * lanes)
    wuv = ukv[:, :, MLA_NOPE:].reshape(MLA_KV_LORA, MLA_WIDTH).T
    ekr = np.zeros((lanes, MLA_HEADS * lanes), np.float32)
    for hh in range(MLA_HEADS):
        ekr[np.arange(MLA_ROPE), hh * lanes + MLA_NOPE + np.arange(MLA_ROPE)] = 1.0
    return {
        "wnat": wnat, "wt": wt,
        "qn": mla_q_norm[None, :], "kvn": mla_kv_norm[None, :],
        "wuq": wuq, "wuk": wuk, "wuv": wuv,
        "ekr": jnp.asarray(ekr, BF16),
        "woa": w_oa.T.astype(BF16), "wob": w_ob.T.astype(BF16), "wout": w_out.astype(BF16),
        "ln_g": ln_g[None, :], "ln_b": ln_b[None, :],
    }


def kernel(x, c, ctx, c_ctx, w_mod, b_mod, w_in, mla_q_norm, mla_kv_norm, w_uq, w_ukv, diff_lambda,
           diff_subln, w_oa, w_ob, w_out, ln_g, ln_b):
    bsz, s, d = x.shape
    n_ctx = ctx.shape[1]
    assert w_mod.shape[0] == DEPTH == 1 and d == D_MODEL and s % GRID_W == 0 and bsz + 1 <= MOD_ROWS

    cc = jnp.concatenate([c, c_ctx[None, :], jnp.zeros((MOD_ROWS - bsz - 1, d), F32)], axis=0)
    mod = _modulation(cc, w_mod[0], b_mod[0][None, :])
    shift, scale, gate = (mod[:bsz, i * d:(i + 1) * d][:, None, :] for i in range(3))
    shift_c, scale_c = (mod[bsz:bsz + 1, i * d:(i + 1) * d][:, None, :] for i in range(2))

    w = _prep_weights(w_in[0], mla_q_norm[0], mla_kv_norm[0], w_uq[0], w_ukv[0], w_oa[0], w_ob[0], w_out[0],
                      ln_g[0], ln_b[0])
    tabn, tabt = _tables(s, rotary=True)
    tabn_c, _ = _tables(n_ctx, rotary=False)
    tabn = jnp.concatenate([tabn, tabn_c], axis=0)

    ka, vta, kd, vtd, qta, qtd, gat, gdt, mt = _projection(x, ctx, scale, shift, scale_c, shift_c, w, tabn, tabt)

    oa = _attention(qta, ka, vta, dv=MLA_V, kernel_fn=_mla_attn_kernel, n_softmax=1, name="attention_mla")
    lam_a = diff_lambda[0][0::2]
    lam_b = diff_lambda[0][1::2]
    od = _attention(qtd, kd, vtd, dv=2 * DIFF_HD, extra=(lam_a, lam_b, diff_subln[0][:, None]),
                    kernel_fn=_diff_attn_kernel, n_softmax=2, name="attention_diff")
    return _output(oa, od, gat, gdt, mt, x, gate, w)
```

```python
import functools
import math

import numpy as np
import jax
import jax.numpy as jnp
from jax import lax
from jax.experimental import pallas as pl
from jax.experimental.pallas import tpu as pltpu

F32 = jnp.float32
BF16 = jnp.bfloat16

D_MODEL = 1024
GRID_W = 64
ROPE_THETA = 10000.0
LN_EPS = 1e-5
RMS_EPS = 1e-6
SUBLN_EPS = 1e-5
MLA_HEADS = 8
MLA_NOPE = 64
MLA_ROPE = 32
MLA_V = 64
MLA_Q_LORA = 384
MLA_KV_LORA = 256
MLA_WIDTH = MLA_HEADS * MLA_V
MLA_SCALE = (MLA_NOPE + MLA_ROPE) ** -0.5
DIFF_HEADS = 4
DIFF_HD = 64
DIFF_WIDTH = DIFF_HEADS * 2 * DIFF_HD
DIFF_SCALE = DIFF_HD ** -0.5
DEPTH = 1
ALPHA = (2 * DEPTH) ** 0.25
LAM_INIT = 0.8 - 0.6 * math.exp(-0.3 * 0)
IN_SPLITS = (MLA_Q_LORA, MLA_KV_LORA, MLA_ROPE, MLA_WIDTH, DIFF_WIDTH, DIFF_WIDTH, DIFF_WIDTH, DIFF_WIDTH,
             2 * D_MODEL)
IN_OFFSETS = tuple(int(o) for o in np.cumsum((0,) + IN_SPLITS))
LOG2E = math.log2(math.e)

V7X_LANES = 128
V7X_VMEM_LIMIT_BYTES = 56 * 2**20
MOD_ROWS = 16
DENOM_ROWS = 16

PROJ_TOKENS = 256
KV_TILE = 256
MLA_Q_TILE = 512
DIFF_Q_TILE = 256
SCORE_SLOTS = 2
OUT_CHUNK = 256

_NAT_CQ = (0, 384)
_NAT_CKV = (384, 640)
_NAT_DK = (640, 1152)
_NAT_KR = (1152, 1280)
_NAT_COLS = 1280
_T_DQ = (0, 512)
_T_DV = (512, 1024)
_T_GA = (1024, 1536)
_T_GD = (1536, 2048)
_T_MERGE = (2048, 4096)
_T_ROWS = 4096


def _nt_dot(a, b):
    return lax.dot_general(a, b, (((1,), (1,)), ((), ())), preferred_element_type=F32)


def _tn_dot(a, b):
    return lax.dot_general(a, b, (((0,), (0,)), ((), ())), preferred_element_type=F32)


def _dot(a, b):
    return jnp.dot(a, b, preferred_element_type=F32)


def _const_spec(shape):
    nd = len(shape)
    return pl.BlockSpec(shape, lambda *_: (0,) * nd)


def _params(semantics):
    return pltpu.CompilerParams(dimension_semantics=semantics, vmem_limit_bytes=V7X_VMEM_LIMIT_BYTES)


def _rope_tables(seq, width):
    h2 = width // 4
    pos = jnp.arange(seq, dtype=jnp.int32)
    row = (pos // GRID_W).astype(F32)
    col = (pos % GRID_W).astype(F32)
    inv = ROPE_THETA ** (-jnp.arange(h2, dtype=F32) / h2)

    def part(p):
        ang = p[:, None] * inv[None, :]
        c, s = lax.optimization_barrier((jnp.cos(ang), jnp.sin(ang)))
        return jnp.concatenate([c, c], -1), jnp.concatenate([-s, s], -1)

    cr, sr = part(row)
    cc, sc = part(col)
    return jnp.concatenate([cr, cc], -1), jnp.concatenate([sr, sc], -1)


def _swap_lanes(x, width):
    h2 = width // 4
    lanes = x.shape[-1]
    lane = lax.broadcasted_iota(jnp.int32, x.shape, x.ndim - 1)
    return jnp.where((lane & h2) == 0, pltpu.roll(x, lanes - h2, x.ndim - 1), pltpu.roll(x, h2, x.ndim - 1))


def _swap_rows(x, width):
    h2 = width // 4
    p = [x[i * h2:(i + 1) * h2] for i in range(4)]
    return jnp.concatenate([p[1], p[0], p[3], p[2]], axis=0)


def _tables(seq, n_ctx):
    narrow = lax.optimization_barrier((*_rope_tables(seq, DIFF_HD), *_rope_tables(seq, MLA_ROPE)))
    c64, s64, c32, s32 = narrow
    tr = jnp.concatenate([c32, s32, c64, s64], axis=-1).T
    ident = lambda t, cos: jnp.concatenate([t, jnp.full((n_ctx, t.shape[1]), 1.0 if cos else 0.0, F32)], axis=0)
    c64, s64, c32, s32 = ident(c64, True), ident(s64, False), ident(c32, True), ident(s32, False)
    pad = jnp.zeros((seq + n_ctx, V7X_LANES - MLA_ROPE), F32)
    nat = jnp.concatenate([c64, c64, s64, s64, c32, pad, s32, pad], axis=-1)
    return nat, tr


def _mod_kernel(c_ref, w_ref, b_ref, o_ref):
    c = c_ref[...]
    s = c * jax.nn.sigmoid(c)
    o_ref[...] = jnp.dot(s, w_ref[...], precision=lax.Precision.HIGHEST,
                         preferred_element_type=F32) + b_ref[...]


def _modulation(cc, w_mod, b_mod):
    rows, d = cc.shape
    n = w_mod.shape[1]
    return pl.pallas_call(
        _mod_kernel,
        grid=(n // d,),
        in_specs=[pl.BlockSpec((rows, d), lambda j: (0, 0)),
                  pl.BlockSpec((d, d), lambda j: (0, j)),
                  pl.BlockSpec((1, d), lambda j: (0, j))],
        out_specs=pl.BlockSpec((rows, d), lambda j: (0, j)),
        out_shape=jax.ShapeDtypeStruct((rows, n), F32),
        compiler_params=_params(("arbitrary",)),
        name="modulation",
    )(cc, w_mod, b_mod)


def _rms_rows(c, w):
    return c * lax.rsqrt(jnp.mean(c * c, axis=-1, keepdims=True) + RMS_EPS) * w


def _proj_kernel(x_ref, ctx_ref, scale_ref, shift_ref, scale_c_ref, shift_c_ref, wnat_ref, wt_ref, qn_ref,
                 kvn_ref, wuq_ref, wuk_ref, ekr_ref, wuv_ref, tabn_ref, tabt_ref,
                 ka_ref, vta_ref, kd_ref, vtd_ref, qta_ref, qtd_ref, gat_ref, gdt_ref, mt_ref, *, n_ctx_steps):
    n_kv, tk = vta_ref.shape[1], vta_ref.shape[-1]
    lanes = V7X_LANES
    step = pl.program_id(1)
    is_ctx = step < n_ctx_steps

    tok = jnp.where(is_ctx, ctx_ref[0], x_ref[0])
    scale = jnp.where(is_ctx, scale_c_ref[0], scale_ref[0])
    shift = jnp.where(is_ctx, shift_c_ref[0], shift_ref[0])
    h = (tok * (1.0 + scale) + shift).astype(BF16)

    mt_ref[0] = jax.nn.sigmoid(_nt_dot(wt_ref[_T_MERGE[0]:_T_MERGE[1], :], h)).astype(BF16)
    ga = _nt_dot(wt_ref[_T_GA[0]:_T_GA[1], :], h)
    gat_ref[0] = (ga * jax.nn.sigmoid(ga)).astype(BF16)
    gd = _nt_dot(wt_ref[_T_GD[0]:_T_GD[1], :], h)
    gdt_ref[0] = (gd * jax.nn.sigmoid(gd)).astype(BF16)

    pn = _dot(h, wnat_ref[...])

    ckv = _rms_rows(pn[:, _NAT_CKV[0]:_NAT_CKV[1]], kvn_ref[...]).astype(BF16)
    kr = pn[:, _NAT_KR[0]:_NAT_KR[1]]
    kr = kr * tabn_ref[:, 2 * lanes:3 * lanes] + _swap_lanes(kr, MLA_ROPE) * tabn_ref[:, 3 * lanes:4 * lanes]
    ka = _dot(ckv, wuk_ref[...]) + _dot(kr.astype(BF16), ekr_ref[...])
    ka_ref[0] = ka.astype(BF16)
    vta = _nt_dot(wuv_ref[...], ckv).astype(BF16)
    for j in range(n_kv):
        vta_ref[0, j] = vta[:, j * tk:(j + 1) * tk]

    for hh in range(DIFF_HEADS):
        a = _NAT_DK[0] + hh * lanes
        kd = pn[:, a:a + lanes]
        kd = kd * tabn_ref[:, 0:lanes] + _swap_lanes(kd, DIFF_HD) * tabn_ref[:, lanes:2 * lanes]
        kd_ref[0, :, hh * lanes:(hh + 1) * lanes] = kd.astype(BF16)

    vtd = _nt_dot(wt_ref[_T_DV[0]:_T_DV[1], :], h).astype(BF16)
    for j in range(n_kv):
        vtd_ref[0, j] = vtd[:, j * tk:(j + 1) * tk]


    cq = _rms_rows(pn[:, _NAT_CQ[0]:_NAT_CQ[1]], qn_ref[...]).astype(BF16)
    qta = _nt_dot(wuq_ref[...], cq) * (MLA_SCALE * LOG2E)
    c32, s32 = tabt_ref[0:MLA_ROPE, :], tabt_ref[MLA_ROPE:2 * MLA_ROPE, :]
    for hh in range(MLA_HEADS):
        base = hh * lanes
        r = qta[base + MLA_NOPE:base + MLA_NOPE + MLA_ROPE]
        r = r * c32 + _swap_rows(r, MLA_ROPE) * s32
        head = jnp.concatenate([qta[base:base + MLA_NOPE], r, qta[base + MLA_NOPE + MLA_ROPE:base + lanes]],
                               axis=0)
        qta_ref[0, hh, 0] = head.astype(BF16)

    qtd = _nt_dot(wt_ref[_T_DQ[0]:_T_DQ[1], :], h) * (DIFF_SCALE * LOG2E)
    c64 = tabt_ref[2 * MLA_ROPE:2 * MLA_ROPE + DIFF_HD, :]
    s64 = tabt_ref[2 * MLA_ROPE + DIFF_HD:2 * MLA_ROPE + 2 * DIFF_HD, :]
    for hh in range(DIFF_HEADS):
        halves = []
        for half in range(2):
            blk = qtd[(2 * hh + half) * DIFF_HD:(2 * hh + half + 1) * DIFF_HD]
            halves.append(blk * c64 + _swap_rows(blk, DIFF_HD) * s64)
        qtd_ref[0, hh, 0] = jnp.concatenate(halves, axis=0).astype(BF16)


def _projection(x, ctx, scale, shift, scale_c, shift_c, w, tabn, tabt):
    bsz, s, d = x.shape
    c = ctx.shape[1]
    ts, tk = PROJ_TOKENS, KV_TILE
    assert s % ts == 0 and c % ts == 0 and ts % tk == 0
    n_lat, n_ctx, n_kv = s // ts, c // ts, ts // tk
    n_total = s + c

    lat = lambda i: jnp.maximum(i - n_ctx, 0)
    kvb = lambda i: jnp.where(i < n_ctx, n_lat + i, i - n_ctx)
    consts = [w[name] for name in ("wnat", "wt", "qn", "kvn", "wuq", "wuk", "ekr", "wuv")]
    in_arrays = [x, ctx, scale, shift, scale_c, shift_c, *consts, tabn, tabt]
    in_specs = [pl.BlockSpec((1, ts, d), lambda b, i: (b, lat(i), 0)),
                pl.BlockSpec((1, ts, d), lambda b, i: (b, jnp.minimum(i, n_ctx - 1), 0)),
                pl.BlockSpec((1, 1, d), lambda b, i: (b, 0, 0)), pl.BlockSpec((1, 1, d), lambda b, i: (b, 0, 0)),
                pl.BlockSpec((1, 1, d), lambda b, i: (0, 0, 0)), pl.BlockSpec((1, 1, d), lambda b, i: (0, 0, 0)),
                *[_const_spec(a.shape) for a in consts],
                pl.BlockSpec((ts, tabn.shape[1]), lambda b, i: (kvb(i), 0)),
                pl.BlockSpec((tabt.shape[0], ts), lambda b, i: (0, lat(i)))]

    hp = MLA_HEADS * V7X_LANES
    feat = lambda rows: (jax.ShapeDtypeStruct((bsz, rows, s), BF16),
                         pl.BlockSpec((1, rows, ts), lambda b, i: (b, 0, lat(i))))
    def qtile(heads, q_tile):
        tq = min(q_tile, s)
        assert tq % ts == 0 and s % tq == 0
        per_q = tq // ts
        return (jax.ShapeDtypeStruct((bsz, heads, s // tq, V7X_LANES, tq), BF16),
                pl.BlockSpec((1, heads, 1, V7X_LANES, ts), lambda b, i: (b, 0, lat(i) // per_q, 0, lat(i) % per_q)))

    outs = [(jax.ShapeDtypeStruct((bsz, n_total, hp), BF16),
             pl.BlockSpec((1, ts, hp), lambda b, i: (b, kvb(i), 0))),
            (jax.ShapeDtypeStruct((bsz, n_total // tk, MLA_WIDTH, tk), BF16),
             pl.BlockSpec((1, n_kv, MLA_WIDTH, tk), lambda b, i: (b, kvb(i), 0, 0))),
            (jax.ShapeDtypeStruct((bsz, n_total, DIFF_WIDTH), BF16),
             pl.BlockSpec((1, ts, DIFF_WIDTH), lambda b, i: (b, kvb(i), 0))),
            (jax.ShapeDtypeStruct((bsz, n_total // tk, DIFF_WIDTH, tk), BF16),
             pl.BlockSpec((1, n_kv, DIFF_WIDTH, tk), lambda b, i: (b, kvb(i), 0, 0))),
            qtile(MLA_HEADS, MLA_Q_TILE), qtile(DIFF_HEADS, DIFF_Q_TILE),
            feat(MLA_WIDTH), feat(DIFF_WIDTH), feat(2 * D_MODEL)]
    return pl.pallas_call(
        functools.partial(_proj_kernel, n_ctx_steps=n_ctx),
        grid=(bsz, n_ctx + n_lat),
        in_specs=in_specs,
        out_specs=[o[1] for o in outs],
        out_shape=[o[0] for o in outs],
        compiler_params=_params(("parallel", "arbitrary")),
        name="projection",
    )(*in_arrays)


def _ones_rows(tk):
    rows = lax.broadcasted_iota(jnp.int32, (DENOM_ROWS, tk), 0)
    return jnp.where(rows == 0, 1.0, 0.0).astype(BF16)


def _softmax_stage(s, vts, ones, stats):
    m_new = jnp.max(s, axis=0, keepdims=True)
    if stats is not None:
        m_old, acc_old = stats
        m_new = jnp.maximum(m_old, m_new)
        alpha = jnp.exp2(m_old - m_new)
    p = jnp.exp2(s - m_new).astype(BF16)
    vt_all = jnp.concatenate([jnp.concatenate([vt, ones], axis=0) for vt in vts], axis=1)
    pv = _dot(vt_all, p)
    return m_new, (pv if stats is None else alpha * acc_old + pv)


def _normalised(stats, dv):
    acc = stats[1]
    return acc[:dv] / acc[dv:dv + 1]


def _kv_group(n_tiles):
    return next(g for g in (3, 2, 1) if n_tiles % g == 0)


def _attention_pipeline(n_q, n_stage, scores, process, finish):
    scores(0, 0, 0)

    def body(qi, carry):
        nxt = jnp.minimum(qi + 1, n_q - 1)
        stats = None
        for j in range(n_stage):
            slot = j % SCORE_SLOTS
            last = j == n_stage - 1
            nq, nj = (nxt, 0) if last else (qi, j + 1)
            if nj % SCORE_SLOTS != slot:
                scores(nq, nj, nj % SCORE_SLOTS)
                stats = process(j, slot, stats)
            else:
                stats = process(j, slot, stats)
                scores(nq, nj, nj % SCORE_SLOTS)
        finish(qi, stats)
        return carry

    lax.fori_loop(0, n_q, body, 0)


def _mla_attn_kernel(q_ref, k_ref, vt_ref, o_ref, s_sc):
    n_q = q_ref.shape[2]
    n_tiles, dv, tk = vt_ref.shape[1:]
    g = _kv_group(n_tiles)
    rows = g * tk
    ones = _ones_rows(tk)

    def scores(qi, j, slot):
        s_sc[slot, 0] = _dot(k_ref[0, j * rows:(j + 1) * rows, :], q_ref[0, 0, qi])

    def process(j, slot, stats):
        return _softmax_stage(s_sc[slot, 0], [vt_ref[0, j * g + i] for i in range(g)], ones, stats)

    def finish(qi, stats):
        o_ref[0, 0, qi] = _normalised(stats, dv).astype(o_ref.dtype)

    _attention_pipeline(n_q, n_tiles // g, scores, process, finish)


def _diff_attn_kernel(q_ref, k_ref, vt_ref, lam_a_ref, lam_b_ref, subln_ref, o_ref, s_sc):
    n_q = q_ref.shape[2]
    n_tiles, dv, tk = vt_ref.shape[1:]
    g = _kv_group(n_tiles)
    rows = g * tk
    ones = _ones_rows(tk)
    zeros = jnp.zeros((DIFF_HD, q_ref.shape[-1]), q_ref.dtype)

    e = jnp.exp(jnp.sum(lam_a_ref[...] * lam_b_ref[...], axis=1, keepdims=True))
    sign = jnp.where(lax.broadcasted_iota(jnp.int32, e.shape, 0) == 0, 1.0, -1.0)
    lam = jnp.sum(e * sign, axis=0, keepdims=True) + LAM_INIT

    def scores(qi, j, slot):
        q = q_ref[0, 0, qi]
        k = k_ref[0, j * rows:(j + 1) * rows, :]
        s_sc[slot, 0] = _dot(k, jnp.concatenate([q[:DIFF_HD], zeros], axis=0))
        s_sc[slot, 1] = _dot(k, jnp.concatenate([zeros, q[DIFF_HD:]], axis=0))

    def process(j, slot, stats):
        vts = [vt_ref[0, j * g + i] for i in range(g)]
        st1, st2 = stats if stats is not None else (None, None)
        return _softmax_stage(s_sc[slot, 0], vts, ones, st1), _softmax_stage(s_sc[slot, 1], vts, ones, st2)

    def finish(qi, stats):
        o = _normalised(stats[0], dv) - lam * _normalised(stats[1], dv)
        o = o * lax.rsqrt(jnp.mean(o * o, axis=0, keepdims=True) + SUBLN_EPS)
        o_ref[0, 0, qi] = ((o * subln_ref[...]) * (1.0 - LAM_INIT)).astype(o_ref.dtype)

    _attention_pipeline(n_q, n_tiles // g, scores, process, finish)


def _attention(qt, k, vt, *, dv, extra=(), kernel_fn, n_softmax, name):
    bsz, heads, n_q, lanes, tq = qt.shape
    t = k.shape[1]
    n_tiles, tk = vt.shape[1], vt.shape[-1]
    assert n_tiles * tk == t and lanes == V7X_LANES
    in_specs = [pl.BlockSpec((1, 1, n_q, lanes, tq), lambda b, h: (b, h, 0, 0, 0)),
                pl.BlockSpec((1, t, lanes), lambda b, h: (b, 0, h)),
                pl.BlockSpec((1, n_tiles, dv, tk), lambda b, h: (b, 0, h, 0))]
    in_specs += [_const_spec(a.shape) for a in extra]
    rows = _kv_group(n_tiles) * tk
    scratch = [pltpu.VMEM((SCORE_SLOTS, n_softmax, rows, tq), F32)]
    return pl.pallas_call(
        kernel_fn,
        grid=(bsz, heads),
        in_specs=in_specs,
        out_specs=pl.BlockSpec((1, 1, n_q, dv, tq), lambda b, h: (b, h, 0, 0, 0)),
        out_shape=jax.ShapeDtypeStruct((bsz, heads, n_q, dv, tq), BF16),
        scratch_shapes=scratch,
        compiler_params=_params(("parallel", "parallel")),
        name=name,
    )(qt, k, vt, *extra)


def _out_kernel(oa_ref, od_ref, ga_ref, gd_ref, m_ref, x_ref, gate_ref, woa_ref, wob_ref, wout_ref,
                g_ref, b_ref, o_ref):
    tf = x_ref.shape[1]
    chunk = min(OUT_CHUNK, tf)

    def heads_major(ref, c0):
        tq = ref.shape[-1]
        t, off = c0 // tq, c0 % tq
        return jnp.concatenate([ref[0, hh, t, :, off:off + chunk] for hh in range(ref.shape[1])], axis=0)

    for c0 in range(0, tf, chunk):
        sl = slice(c0, c0 + chunk)
        ua = (heads_major(oa_ref, c0).astype(F32) * ga_ref[0, :, sl].astype(F32)).astype(BF16)
        ud = (heads_major(od_ref, c0).astype(F32) * gd_ref[0, :, sl].astype(F32)).astype(BF16)
        ya = _dot(woa_ref[...], ua)
        yd = _dot(wob_ref[...], ud)
        mix = m_ref[0, :D_MODEL, sl].astype(F32) * ya + m_ref[0, D_MODEL:, sl].astype(F32) * yd
        y = _tn_dot(mix.astype(BF16), wout_ref[...])
        z = ALPHA * x_ref[0, sl, :] + gate_ref[0] * y
        mu = jnp.mean(z, axis=-1, keepdims=True)
        zc = z - mu
        var = jnp.mean(zc * zc, axis=-1, keepdims=True)
        o_ref[0, sl, :] = zc * lax.rsqrt(var + LN_EPS) * g_ref[...] + b_ref[...]


def _output(oa, od, ga, gd, mt, x, gate, w):
    bsz, s, d = x.shape
    tf = max(oa.shape[-1], od.shape[-1])
    feat = lambda rows: pl.BlockSpec((1, rows, tf), lambda b, i: (b, 0, i))
    qtile = lambda a: pl.BlockSpec((1, a.shape[1], tf // a.shape[-1], a.shape[3], a.shape[-1]),
                                   lambda b, i: (b, 0, i, 0, 0))
    return pl.pallas_call(
        _out_kernel,
        grid=(bsz, s // tf),
        in_specs=[qtile(oa), qtile(od), feat(MLA_WIDTH), feat(DIFF_WIDTH), feat(2 * d),
                  pl.BlockSpec((1, tf, d), lambda b, i: (b, i, 0)),
                  pl.BlockSpec((1, 1, d), lambda b, i: (b, 0, 0)),
                  _const_spec(w["woa"].shape), _const_spec(w["wob"].shape), _const_spec(w["wout"].shape),
                  _const_spec(w["ln_g"].shape), _const_spec(w["ln_b"].shape)],
        out_specs=pl.BlockSpec((1, tf, d), lambda b, i: (b, i, 0)),
        out_shape=jax.ShapeDtypeStruct((bsz, s, d), F32),
        compiler_params=_params(("parallel", "parallel")),
        name="output",
    )(oa, od, ga, gd, mt, x, gate, w["woa"], w["wob"], w["wout"], w["ln_g"], w["ln_b"])


def _prep_weights(w_in, mla_q_norm, mla_kv_norm, w_uq, w_ukv, w_oa, w_ob, w_out, ln_g, ln_b):
    o = IN_OFFSETS
    w_in, w_uq, w_ukv = w_in.astype(BF16), w_uq.astype(BF16), w_ukv.astype(BF16)
    cols = lambda i: w_in[:, o[i]:o[i + 1]]
    w_cq, w_ckv, w_kr, w_ga, w_dq, w_dk, w_dv, w_gd, w_mg = (cols(i) for i in range(9))
    lanes = V7X_LANES
    pad_kr = jnp.zeros((D_MODEL, lanes - MLA_ROPE), BF16)
    wnat = jnp.concatenate([w_cq, w_ckv, w_dk, w_kr, pad_kr], axis=1)
    assert wnat.shape[1] == _NAT_COLS
    wt = jnp.concatenate([w_dq, w_dv, w_ga, w_gd, w_mg], axis=1).T
    assert wt.shape[0] == _T_ROWS

    uq = w_uq.reshape(MLA_Q_LORA, MLA_HEADS, MLA_NOPE + MLA_ROPE)
    uq = jnp.pad(uq, ((0, 0), (0, 0), (0, lanes - MLA_NOPE - MLA_ROPE)))
    wuq = uq.reshape(MLA_Q_LORA, MLA_HEADS * lanes).T
    ukv = w_ukv.reshape(MLA_KV_LORA, MLA_HEADS, MLA_NOPE + MLA_V)
    wuk = jnp.pad(ukv[:, :, :MLA_NOPE], ((0, 0), (0, 0), (0, lanes - MLA_NOPE)))
    wuk = wuk.reshape(MLA_KV_LORA, MLA_HEADS * lanes)
    wuv = ukv[:, :, MLA_NOPE:].reshape(MLA_KV_LORA, MLA_WIDTH).T
    ekr = np.zeros((lanes, MLA_HEADS * lanes), np.float32)
    for hh in range(MLA_HEADS):
        ekr[np.arange(MLA_ROPE), hh * lanes + MLA_NOPE + np.arange(MLA_ROPE)] = 1.0
    return {
        "wnat": wnat, "wt": wt,
        "qn": mla_q_norm[None, :], "kvn": mla_kv_norm[None, :],
        "wuq": wuq, "wuk": wuk, "wuv": wuv,
        "ekr": jnp.asarray(ekr, BF16),
        "woa": w_oa.T.astype(BF16), "wob": w_ob.T.astype(BF16), "wout": w_out.astype(BF16),
        "ln_g": ln_g[None, :], "ln_b": ln_b[None, :],
    }


def kernel(x, c, ctx, c_ctx, w_mod, b_mod, w_in, mla_q_norm, mla_kv_norm, w_uq, w_ukv, diff_lambda,
           diff_subln, w_oa, w_ob, w_out, ln_g, ln_b):
    bsz, s, d = x.shape
    n_ctx = ctx.shape[1]
    assert w_mod.shape[0] == DEPTH == 1 and d == D_MODEL and s % GRID_W == 0 and bsz + 1 <= MOD_ROWS

    cc = jnp.concatenate([c, c_ctx[None, :], jnp.zeros((MOD_ROWS - bsz - 1, d), F32)], axis=0)
    mod = _modulation(cc, w_mod[0], b_mod[0][None, :])
    shift, scale, gate = (mod[:bsz, i * d:(i + 1) * d][:, None, :] for i in range(3))
    shift_c, scale_c = (mod[bsz:bsz + 1, i * d:(i + 1) * d][:, None, :] for i in range(2))

    w = _prep_weights(w_in[0], mla_q_norm[0], mla_kv_norm[0], w_uq[0], w_ukv[0], w_oa[0], w_ob[0], w_out[0],
                      ln_g[0], ln_b[0])
    tabn, tabt = _tables(s, n_ctx)

    ka, vta, kd, vtd, qta, qtd, gat, gdt, mt = _projection(x, ctx, scale, shift, scale_c, shift_c, w, tabn, tabt)

    oa = _attention(qta, ka, vta, dv=MLA_V, kernel_fn=_mla_attn_kernel, n_softmax=1, name="attention_mla")
    lam_a = diff_lambda[0][0::2]
    lam_b = diff_lambda[0][1::2]
    od = _attention(qtd, kd, vtd, dv=2 * DIFF_HD, extra=(lam_a, lam_b, diff_subln[0][:, None]),
                    kernel_fn=_diff_attn_kernel, n_softmax=2, name="attention_diff")
    return _output(oa, od, gat, gdt, mt, x, gate, w)
```

```python
import functools
import math

import numpy as np
import jax
import jax.numpy as jnp
from jax import lax
from jax.experimental import pallas as pl
from jax.experimental.pallas import tpu as pltpu

F32 = jnp.float32
BF16 = jnp.bfloat16

D_MODEL = 1024
GRID_W = 64
ROPE_THETA = 10000.0
LN_EPS = 1e-5
RMS_EPS = 1e-6
SUBLN_EPS = 1e-5
MLA_HEADS = 8
MLA_NOPE = 64
MLA_ROPE = 32
MLA_V = 64
MLA_Q_LORA = 384
MLA_KV_LORA = 256
MLA_WIDTH = MLA_HEADS * MLA_V
MLA_SCALE = (MLA_NOPE + MLA_ROPE) ** -0.5
DIFF_HEADS = 4
DIFF_HD = 64
DIFF_WIDTH = DIFF_HEADS * 2 * DIFF_HD
DIFF_SCALE = DIFF_HD ** -0.5
DEPTH = 1
ALPHA = (2 * DEPTH) ** 0.25
LAM_INIT = 0.8 - 0.6 * math.exp(-0.3 * 0)
IN_SPLITS = (MLA_Q_LORA, MLA_KV_LORA, MLA_ROPE, MLA_WIDTH, DIFF_WIDTH, DIFF_WIDTH, DIFF_WIDTH, DIFF_WIDTH,
             2 * D_MODEL)
IN_OFFSETS = tuple(int(o) for o in np.cumsum((0,) + IN_SPLITS))
LOG2E = math.log2(math.e)

V7X_LANES = 128
V7X_VMEM_LIMIT_BYTES = 56 * 2**20
MOD_ROWS = 16
DENOM_ROWS = 16

PROJ_TOKENS = 256
KV_TILE = 256
MLA_Q_TILE = 512
DIFF_Q_TILE = 256
SCORE_SLOTS = 2
OUT_CHUNK = 256

_NAT_CQ = (0, 384)
_NAT_CKV = (384, 640)
_NAT_DK = (640, 1152)
_NAT_KR = (1152, 1280)
_NAT_COLS = 1280
_T_DQ = (0, 512)
_T_DV = (512, 1024)
_T_GA = (1024, 1536)
_T_GD = (1536, 2048)
_T_MERGE = (2048, 4096)
_T_ROWS = 4096


def _nt_dot(a, b):
    return lax.dot_general(a, b, (((1,), (1,)), ((), ())), preferred_element_type=F32)


def _tn_dot(a, b):
    return lax.dot_general(a, b, (((0,), (0,)), ((), ())), preferred_element_type=F32)


def _dot(a, b):
    return jnp.dot(a, b, preferred_element_type=F32)


def _const_spec(shape):
    nd = len(shape)
    return pl.BlockSpec(shape, lambda *_: (0,) * nd, pipeline_mode=pl.Buffered(1))


def _params(semantics):
    return pltpu.CompilerParams(dimension_semantics=semantics, vmem_limit_bytes=V7X_VMEM_LIMIT_BYTES)


def _rope_tables(seq, width):
    h2 = width // 4
    pos = jnp.arange(seq, dtype=jnp.int32)
    row = (pos // GRID_W).astype(F32)
    col = (pos % GRID_W).astype(F32)
    inv = ROPE_THETA ** (-jnp.arange(h2, dtype=F32) / h2)

    def part(p):
        ang = p[:, None] * inv[None, :]
        c, s = lax.optimization_barrier((jnp.cos(ang), jnp.sin(ang)))
        return jnp.concatenate([c, c], -1), jnp.concatenate([-s, s], -1)

    cr, sr = part(row)
    cc, sc = part(col)
    return jnp.concatenate([cr, cc], -1), jnp.concatenate([sr, sc], -1)


def _swap_lanes(x, width):
    h2 = width // 4
    lanes = x.shape[-1]
    lane = lax.broadcasted_iota(jnp.int32, x.shape, x.ndim - 1)
    return jnp.where((lane & h2) == 0, pltpu.roll(x, lanes - h2, x.ndim - 1), pltpu.roll(x, h2, x.ndim - 1))


def _swap_rows(x, width):
    h2 = width // 4
    p = [x[i * h2:(i + 1) * h2] for i in range(4)]
    return jnp.concatenate([p[1], p[0], p[3], p[2]], axis=0)


def _tables(seq, n_ctx):
    narrow = lax.optimization_barrier((*_rope_tables(seq, DIFF_HD), *_rope_tables(seq, MLA_ROPE)))
    c64, s64, c32, s32 = narrow
    tr = jnp.concatenate([c32, s32, c64, s64], axis=-1).T
    ident = lambda t, cos: jnp.concatenate([t, jnp.full((n_ctx, t.shape[1]), 1.0 if cos else 0.0, F32)], axis=0)
    c64, s64, c32, s32 = ident(c64, True), ident(s64, False), ident(c32, True), ident(s32, False)
    pad = jnp.zeros((seq + n_ctx, V7X_LANES - MLA_ROPE), F32)
    nat = jnp.concatenate([c64, c64, s64, s64, c32, pad, s32, pad], axis=-1)
    return nat, tr


def _mod_kernel(c_ref, w_ref, b_ref, o_ref):
    c = c_ref[...]
    s = c * jax.nn.sigmoid(c)
    o_ref[...] = jnp.dot(s, w_ref[...], precision=lax.Precision.HIGHEST,
                         preferred_element_type=F32) + b_ref[...]


def _modulation(cc, w_mod, b_mod):
    rows, d = cc.shape
    n = w_mod.shape[1]
    return pl.pallas_call(
        _mod_kernel,
        grid=(n // d,),
        in_specs=[pl.BlockSpec((rows, d), lambda j: (0, 0)),
                  pl.BlockSpec((d, d), lambda j: (0, j)),
                  pl.BlockSpec((1, d), lambda j: (0, j))],
        out_specs=pl.BlockSpec((rows, d), lambda j: (0, j)),
        out_shape=jax.ShapeDtypeStruct((rows, n), F32),
        compiler_params=_params(("arbitrary",)),
        name="modulation",
    )(cc, w_mod, b_mod)


def _rms_rows(c, w):
    return c * lax.rsqrt(jnp.mean(c * c, axis=-1, keepdims=True) + RMS_EPS) * w


def _proj_kernel(x_ref, ctx_ref, scale_ref, shift_ref, scale_c_ref, shift_c_ref, wnat_ref, wt_ref, qn_ref,
                 kvn_ref, wuq_ref, wuk_ref, ekr_ref, wuv_ref, tabn_ref, tabt_ref,
                 ka_ref, vta_ref, kd_ref, vtd_ref, qta_ref, qtd_ref, gat_ref, gdt_ref, mt_ref, *, n_ctx_steps):
    n_kv, tk = vta_ref.shape[1], vta_ref.shape[-1]
    lanes = V7X_LANES
    step = pl.program_id(1)
    is_ctx = step < n_ctx_steps

    tok = jnp.where(is_ctx, ctx_ref[0], x_ref[0])
    scale = jnp.where(is_ctx, scale_c_ref[0], scale_ref[0])
    shift = jnp.where(is_ctx, shift_c_ref[0], shift_ref[0])
    h = (tok * (1.0 + scale) + shift).astype(BF16)

    mt_ref[0] = jax.nn.sigmoid(_nt_dot(wt_ref[_T_MERGE[0]:_T_MERGE[1], :], h)).astype(BF16)
    ga = _nt_dot(wt_ref[_T_GA[0]:_T_GA[1], :], h)
    gat_ref[0] = (ga * jax.nn.sigmoid(ga)).astype(BF16)
    gd = _nt_dot(wt_ref[_T_GD[0]:_T_GD[1], :], h)
    gdt_ref[0] = (gd * jax.nn.sigmoid(gd)).astype(BF16)

    pn = _dot(h, wnat_ref[...])

    ckv = _rms_rows(pn[:, _NAT_CKV[0]:_NAT_CKV[1]], kvn_ref[...]).astype(BF16)
    kr = pn[:, _NAT_KR[0]:_NAT_KR[1]]
    kr = kr * tabn_ref[:, 2 * lanes:3 * lanes] + _swap_lanes(kr, MLA_ROPE) * tabn_ref[:, 3 * lanes:4 * lanes]
    ka = _dot(ckv, wuk_ref[...]) + _dot(kr.astype(BF16), ekr_ref[...])
    ka_ref[0] = ka.astype(BF16)
    vta = _nt_dot(wuv_ref[...], ckv).astype(BF16)
    for j in range(n_kv):
        vta_ref[0, j] = vta[:, j * tk:(j + 1) * tk]

    for hh in range(DIFF_HEADS):
        a = _NAT_DK[0] + hh * lanes
        kd = pn[:, a:a + lanes]
        kd = kd * tabn_ref[:, 0:lanes] + _swap_lanes(kd, DIFF_HD) * tabn_ref[:, lanes:2 * lanes]
        kd_ref[0, :, hh * lanes:(hh + 1) * lanes] = kd.astype(BF16)

    vtd = _nt_dot(wt_ref[_T_DV[0]:_T_DV[1], :], h).astype(BF16)
    for j in range(n_kv):
        vtd_ref[0, j] = vtd[:, j * tk:(j + 1) * tk]


    cq = _rms_rows(pn[:, _NAT_CQ[0]:_NAT_CQ[1]], qn_ref[...]).astype(BF16)
    qta = _nt_dot(wuq_ref[...], cq) * (MLA_SCALE * LOG2E)
    c32, s32 = tabt_ref[0:MLA_ROPE, :], tabt_ref[MLA_ROPE:2 * MLA_ROPE, :]
    for hh in range(MLA_HEADS):
        base = hh * lanes
        r = qta[base + MLA_NOPE:base + MLA_NOPE + MLA_ROPE]
        r = r * c32 + _swap_rows(r, MLA_ROPE) * s32
        head = jnp.concatenate([qta[base:base + MLA_NOPE], r, qta[base + MLA_NOPE + MLA_ROPE:base + lanes]],
                               axis=0)
        qta_ref[0, hh, 0] = head.astype(BF16)

    qtd = _nt_dot(wt_ref[_T_DQ[0]:_T_DQ[1], :], h) * (DIFF_SCALE * LOG2E)
    c64 = tabt_ref[2 * MLA_ROPE:2 * MLA_ROPE + DIFF_HD, :]
    s64 = tabt_ref[2 * MLA_ROPE + DIFF_HD:2 * MLA_ROPE + 2 * DIFF_HD, :]
    for hh in range(DIFF_HEADS):
        halves = []
        for half in range(2):
            blk = qtd[(2 * hh + half) * DIFF_HD:(2 * hh + half + 1) * DIFF_HD]
            halves.append(blk * c64 + _swap_rows(blk, DIFF_HD) * s64)
        qtd_ref[0, hh, 0] = jnp.concatenate(halves, axis=0).astype(BF16)


def _projection(x, ctx, scale, shift, scale_c, shift_c, w, tabn, tabt):
    bsz, s, d = x.shape
    c = ctx.shape[1]
    ts, tk = PROJ_TOKENS, KV_TILE
    assert s % ts == 0 and c % ts == 0 and ts % tk == 0
    n_lat, n_ctx, n_kv = s // ts, c // ts, ts // tk
    n_total = s + c

    lat = lambda i: jnp.maximum(i - n_ctx, 0)
    kvb = lambda i: jnp.where(i < n_ctx, n_lat + i, i - n_ctx)
    consts = [w[name] for name in ("wnat", "wt", "qn", "kvn", "wuq", "wuk", "ekr", "wuv")]
    in_arrays = [x, ctx, scale, shift, scale_c, shift_c, *consts, tabn, tabt]
    in_specs = [pl.BlockSpec((1, ts, d), lambda b, i: (b, lat(i), 0)),
                pl.BlockSpec((1, ts, d), lambda b, i: (b, jnp.minimum(i, n_ctx - 1), 0)),
                pl.BlockSpec((1, 1, d), lambda b, i: (b, 0, 0)), pl.BlockSpec((1, 1, d), lambda b, i: (b, 0, 0)),
                pl.BlockSpec((1, 1, d), lambda b, i: (0, 0, 0)), pl.BlockSpec((1, 1, d), lambda b, i: (0, 0, 0)),
                *[_const_spec(a.shape) for a in consts],
                pl.BlockSpec((ts, tabn.shape[1]), lambda b, i: (kvb(i), 0)),
                pl.BlockSpec((tabt.shape[0], ts), lambda b, i: (0, lat(i)))]

    hp = MLA_HEADS * V7X_LANES
    feat = lambda rows: (jax.ShapeDtypeStruct((bsz, rows, s), BF16),
                         pl.BlockSpec((1, rows, ts), lambda b, i: (b, 0, lat(i))))
    def qtile(heads, q_tile):
        tq = min(q_tile, s)
        assert tq % ts == 0 and s % tq == 0
        per_q = tq // ts
        return (jax.ShapeDtypeStruct((bsz, heads, s // tq, V7X_LANES, tq), BF16),
                pl.BlockSpec((1, heads, 1, V7X_LANES, ts), lambda b, i: (b, 0, lat(i) // per_q, 0, lat(i) % per_q)))

    outs = [(jax.ShapeDtypeStruct((bsz, n_total, hp), BF16),
             pl.BlockSpec((1, ts, hp), lambda b, i: (b, kvb(i), 0))),
            (jax.ShapeDtypeStruct((bsz, n_total // tk, MLA_WIDTH, tk), BF16),
             pl.BlockSpec((1, n_kv, MLA_WIDTH, tk), lambda b, i: (b, kvb(i), 0, 0))),
            (jax.ShapeDtypeStruct((bsz, n_total, DIFF_WIDTH), BF16),
             pl.BlockSpec((1, ts, DIFF_WIDTH), lambda b, i: (b, kvb(i), 0))),
            (jax.ShapeDtypeStruct((bsz, n_total // tk, DIFF_WIDTH, tk), BF16),
             pl.BlockSpec((1, n_kv, DIFF_WIDTH, tk), lambda b, i: (b, kvb(i), 0, 0))),
            qtile(MLA_HEADS, MLA_Q_TILE), qtile(DIFF_HEADS, DIFF_Q_TILE),
            feat(MLA_WIDTH), feat(DIFF_WIDTH), feat(2 * D_MODEL)]
    return pl.pallas_call(
        functools.partial(_proj_kernel, n_ctx_steps=n_ctx),
        grid=(bsz, n_ctx + n_lat),
        in_specs=in_specs,
        out_specs=[o[1] for o in outs],
        out_shape=[o[0] for o in outs],
        compiler_params=_params(("parallel", "arbitrary")),
        name="projection",
    )(*in_arrays)


def _ones_rows(tk):
    rows = lax.broadcasted_iota(jnp.int32, (DENOM_ROWS, tk), 0)
    return jnp.where(rows == 0, 1.0, 0.0).astype(BF16)


def _softmax_stage(s, vts, ones, stats):
    m_new = jnp.max(s, axis=0, keepdims=True)
    if stats is not None:
        m_old, acc_old = stats
        m_new = jnp.maximum(m_old, m_new)
        alpha = jnp.exp2(m_old - m_new)
    p = jnp.exp2(s - m_new).astype(BF16)
    vt_all = jnp.concatenate([jnp.concatenate([vt, ones], axis=0) for vt in vts], axis=1)
    pv = _dot(vt_all, p)
    return m_new, (pv if stats is None else alpha * acc_old + pv)


def _normalised(stats, dv):
    acc = stats[1]
    return acc[:dv] / acc[dv:dv + 1]


def _kv_group(n_tiles):
    return next(g for g in (3, 2, 1) if n_tiles % g == 0)


def _attention_pipeline(n_q, n_stage, scores, process, finish):
    scores(0, 0, 0)

    def body(qi, carry):
        nxt = jnp.minimum(qi + 1, n_q - 1)
        stats = None
        for j in range(n_stage):
            slot = j % SCORE_SLOTS
            last = j == n_stage - 1
            nq, nj = (nxt, 0) if last else (qi, j + 1)
            if nj % SCORE_SLOTS != slot:
                scores(nq, nj, nj % SCORE_SLOTS)
                stats = process(j, slot, stats)
            else:
                stats = process(j, slot, stats)
                scores(nq, nj, nj % SCORE_SLOTS)
        finish(qi, stats)
        return carry

    lax.fori_loop(0, n_q, body, 0)


def _mla_attn_kernel(q_ref, k_ref, vt_ref, o_ref, s_sc):
    n_q = q_ref.shape[2]
    n_tiles, dv, tk = vt_ref.shape[1:]
    g = _kv_group(n_tiles)
    rows = g * tk
    ones = _ones_rows(tk)

    def scores(qi, j, slot):
        s_sc[slot, 0] = _dot(k_ref[0, j * rows:(j + 1) * rows, :], q_ref[0, 0, qi])

    def process(j, slot, stats):
        return _softmax_stage(s_sc[slot, 0], [vt_ref[0, j * g + i] for i in range(g)], ones, stats)

    def finish(qi, stats):
        o_ref[0, 0, qi] = _normalised(stats, dv).astype(o_ref.dtype)

    _attention_pipeline(n_q, n_tiles // g, scores, process, finish)


def _diff_attn_kernel(q_ref, k_ref, vt_ref, lam_a_ref, lam_b_ref, subln_ref, o_ref, s_sc):
    n_q = q_ref.shape[2]
    n_tiles, dv, tk = vt_ref.shape[1:]
    g = _kv_group(n_tiles)
    rows = g * tk
    ones = _ones_rows(tk)
    zeros = jnp.zeros((DIFF_HD, q_ref.shape[-1]), q_ref.dtype)

    e = jnp.exp(jnp.sum(lam_a_ref[...] * lam_b_ref[...], axis=1, keepdims=True))
    sign = jnp.where(lax.broadcasted_iota(jnp.int32, e.shape, 0) == 0, 1.0, -1.0)
    lam = jnp.sum(e * sign, axis=0, keepdims=True) + LAM_INIT

    def scores(qi, j, slot):
        q = q_ref[0, 0, qi]
        k = k_ref[0, j * rows:(j + 1) * rows, :]
        s_sc[slot, 0] = _dot(k, jnp.concatenate([q[:DIFF_HD], zeros], axis=0))
        s_sc[slot, 1] = _dot(k, jnp.concatenate([zeros, q[DIFF_HD:]], axis=0))

    def process(j, slot, stats):
        vts = [vt_ref[0, j * g + i] for i in range(g)]
        st1, st2 = stats if stats is not None else (None, None)
        return _softmax_stage(s_sc[slot, 0], vts, ones, st1), _softmax_stage(s_sc[slot, 1], vts, ones, st2)

    def finish(qi, stats):
        o = _normalised(stats[0], dv) - lam * _normalised(stats[1], dv)
        o = o * lax.rsqrt(jnp.mean(o * o, axis=0, keepdims=True) + SUBLN_EPS)
        o_ref[0, 0, qi] = ((o * subln_ref[...]) * (1.0 - LAM_INIT)).astype(o_ref.dtype)

    _attention_pipeline(n_q, n_tiles // g, scores, process, finish)


def _attention(qt, k, vt, *, dv, extra=(), kernel_fn, n_softmax, name):
    bsz, heads, n_q, lanes, tq = qt.shape
    t = k.shape[1]
    n_tiles, tk = vt.shape[1], vt.shape[-1]
    assert n_tiles * tk == t and lanes == V7X_LANES
    in_specs = [pl.BlockSpec((1, 1, n_q, lanes, tq), lambda b, h: (b, h, 0, 0, 0)),
                pl.BlockSpec((1, t, lanes), lambda b, h: (b, 0, h)),
                pl.BlockSpec((1, n_tiles, dv, tk), lambda b, h: (b, 0, h, 0))]
    in_specs += [_const_spec(a.shape) for a in extra]
    rows = _kv_group(n_tiles) * tk
    scratch = [pltpu.VMEM((SCORE_SLOTS, n_softmax, rows, tq), F32)]
    return pl.pallas_call(
        kernel_fn,
        grid=(bsz, heads),
        in_specs=in_specs,
        out_specs=pl.BlockSpec((1, 1, n_q, dv, tq), lambda b, h: (b, h, 0, 0, 0)),
        out_shape=jax.ShapeDtypeStruct((bsz, heads, n_q, dv, tq), BF16),
        scratch_shapes=scratch,
        compiler_params=_params(("parallel", "parallel")),
        name=name,
    )(qt, k, vt, *extra)


def _out_kernel(oa_ref, od_ref, ga_ref, gd_ref, m_ref, x_ref, gate_ref, woa_ref, wob_ref, wout_ref,
                g_ref, b_ref, o_ref):
    tf = x_ref.shape[1]
    chunk = min(OUT_CHUNK, tf)

    def heads_major(ref, c0):
        tq = ref.shape[-1]
        t, off = c0 // tq, c0 % tq
        return jnp.concatenate([ref[0, hh, t, :, off:off + chunk] for hh in range(ref.shape[1])], axis=0)

    for c0 in range(0, tf, chunk):
        sl = slice(c0, c0 + chunk)
        ua = (heads_major(oa_ref, c0).astype(F32) * ga_ref[0, :, sl].astype(F32)).astype(BF16)
        ud = (heads_major(od_ref, c0).astype(F32) * gd_ref[0, :, sl].astype(F32)).astype(BF16)
        ya = _dot(woa_ref[...], ua)
        yd = _dot(wob_ref[...], ud)
        mix = m_ref[0, :D_MODEL, sl].astype(F32) * ya + m_ref[0, D_MODEL:, sl].astype(F32) * yd
        y = _tn_dot(mix.astype(BF16), wout_ref[...])
        z = ALPHA * x_ref[0, sl, :] + gate_ref[0] * y
        mu = jnp.mean(z, axis=-1, keepdims=True)
        zc = z - mu
        var = jnp.mean(zc * zc, axis=-1, keepdims=True)
        o_ref[0, sl, :] = zc * lax.rsqrt(var + LN_EPS) * g_ref[...] + b_ref[...]


def _output(oa, od, ga, gd, mt, x, gate, w):
    bsz, s, d = x.shape
    tf = max(oa.shape[-1], od.shape[-1])
    feat = lambda rows: pl.BlockSpec((1, rows, tf), lambda b, i: (b, 0, i))
    qtile = lambda a: pl.BlockSpec((1, a.shape[1], tf // a.shape[-1], a.shape[3], a.shape[-1]),
                                   lambda b, i: (b, 0, i, 0, 0))
    return pl.pallas_call(
        _out_kernel,
        grid=(bsz, s // tf),
        in_specs=[qtile(oa), qtile(od), feat(MLA_WIDTH), feat(DIFF_WIDTH), feat(2 * d),
                  pl.BlockSpec((1, tf, d), lambda b, i: (b, i, 0)),
                  pl.BlockSpec((1, 1, d), lambda b, i: (b, 0, 0)),
                  _const_spec(w["woa"].shape), _const_spec(w["wob"].shape), _const_spec(w["wout"].shape),
                  _const_spec(w["ln_g"].shape), _const_spec(w["ln_b"].shape)],
        out_specs=pl.BlockSpec((1, tf, d), lambda b, i: (b, i, 0)),
        out_shape=jax.ShapeDtypeStruct((bsz, s, d), F32),
        compiler_params=_params(("parallel", "parallel")),
        name="output",
    )(oa, od, ga, gd, mt, x, gate, w["woa"], w["wob"], w["wout"], w["ln_g"], w["ln_b"])


def _prep_weights(w_in, mla_q_norm, mla_kv_norm, w_uq, w_ukv, w_oa, w_ob, w_out, ln_g, ln_b):
    o = IN_OFFSETS
    w_in, w_uq, w_ukv = w_in.astype(BF16), w_uq.astype(BF16), w_ukv.astype(BF16)
    cols = lambda i: w_in[:, o[i]:o[i + 1]]
    w_cq, w_ckv, w_kr, w_ga, w_dq, w_dk, w_dv, w_gd, w_mg = (cols(i) for i in range(9))
    lanes = V7X_LANES
    pad_kr = jnp.zeros((D_MODEL, lanes - MLA_ROPE), BF16)
    wnat = jnp.concatenate([w_cq, w_ckv, w_dk, w_kr, pad_kr], axis=1)
    assert wnat.shape[1] == _NAT_COLS
    wt = jnp.concatenate([w_dq, w_dv, w_ga, w_gd, w_mg], axis=1).T
    assert wt.shape[0] == _T_ROWS

    uq = w_uq.reshape(MLA_Q_LORA, MLA_HEADS, MLA_NOPE + MLA_ROPE)
    uq = jnp.pad(uq, ((0, 0), (0, 0), (0, lanes - MLA_NOPE - MLA_ROPE)))
    wuq = uq.reshape(MLA_Q_LORA, MLA_HEADS * lanes).T
    ukv = w_ukv.reshape(MLA_KV_LORA, MLA_HEADS, MLA_NOPE + MLA_V)
    wuk = jnp.pad(ukv[:, :, :MLA_NOPE], ((0, 0), (0, 0), (0, lanes - MLA_NOPE)))
    wuk = wuk.reshape(MLA_KV_LORA, MLA_HEADS * lanes)
    wuv = ukv[:, :, MLA_NOPE:].reshape(MLA_KV_LORA, MLA_WIDTH).T
    ekr = np.zeros((lanes, MLA_HEADS * lanes), np.float32)
    for hh in range(MLA_HEADS):
        ekr[np.arange(MLA_ROPE), hh * lanes + MLA_NOPE + np.arange(MLA_ROPE)] = 1.0
    return {
        "wnat": wnat, "wt": wt,
        "qn": mla_q_norm[None, :], "kvn": mla_kv_norm[None, :],
        "wuq": wuq, "wuk": wuk, "wuv": wuv,
        "ekr": jnp.asarray(ekr, BF16),
        "woa": w_oa.T.astype(BF16), "wob": w_ob.T.astype(BF16), "wout": w_out.astype(BF16),
        "ln_g": ln_g[None, :], "ln_b": ln_b[None, :],
    }


def kernel(x, c, ctx, c_ctx, w_mod, b_mod, w_in, mla_q_norm, mla_kv_norm, w_uq, w_ukv, diff_lambda,
           diff_subln, w_oa, w_ob, w_out, ln_g, ln_b):
    bsz, s, d = x.shape
    n_ctx = ctx.shape[1]
    assert w_mod.shape[0] == DEPTH == 1 and d == D_MODEL and s % GRID_W == 0 and bsz + 1 <= MOD_ROWS

    cc = jnp.concatenate([c, c_ctx[None, :], jnp.zeros((MOD_ROWS - bsz - 1, d), F32)], axis=0)
    mod = _modulation(cc, w_mod[0], b_mod[0][None, :])
    shift, scale, gate = (mod[:bsz, i * d:(i + 1) * d][:, None, :] for i in range(3))
    shift_c, scale_c = (mod[bsz:bsz + 1, i * d:(i + 1) * d][:, None, :] for i in range(2))

    w = _prep_weights(w_in[0], mla_q_norm[0], mla_kv_norm[0], w_uq[0], w_ukv[0], w_oa[0], w_ob[0], w_out[0],
                      ln_g[0], ln_b[0])
    tabn, tabt = _tables(s, n_ctx)

    ka, vta, kd, vtd, qta, qtd, gat, gdt, mt = _projection(x, ctx, scale, shift, scale_c, shift_c, w, tabn, tabt)

    oa = _attention(qta, ka, vta, dv=MLA_V, kernel_fn=_mla_attn_kernel, n_softmax=1, name="attention_mla")
    lam_a = diff_lambda[0][0::2]
    lam_b = diff_lambda[0][1::2]
    od = _attention(qtd, kd, vtd, dv=2 * DIFF_HD, extra=(lam_a, lam_b, diff_subln[0][:, None]),
                    kernel_fn=_diff_attn_kernel, n_softmax=2, name="attention_diff")
    return _output(oa, od, gat, gdt, mt, x, gate, w)
```
